```python
import math
import jax, jax.numpy as jnp
from jax import lax
import numpy as np

D_MODEL = 2048
BATCH = 4
SEQ = 2048
DEPTH = 4
DEC_BATCH = 8
DEC_SEQ = 8
PAST_LEN = 16384
PAGE_SIZE = 128

N_MIXERS = 2
N_HGRN_LAYERS = (DEPTH + N_MIXERS - 1) // N_MIXERS
N_DIFF_LAYERS = DEPTH // N_MIXERS
HG_HEADS = 16
HG_DK = D_MODEL // HG_HEADS
HG_DV = D_MODEL // HG_HEADS
HG_WIDTH = HG_HEADS * HG_DK
HG_CHUNK = 64
DF_HEADS = 16
DF_DH = D_MODEL // DF_HEADS // 2
DF_DV = 2 * DF_DH
Q_BLOCK = 128
NUM_BUCKETS = 32
MAX_DISTANCE = 128
N_GROUPS = 4
EXPERTS_PER_GROUP = 8
N_EXPERTS = N_GROUPS * EXPERTS_PER_GROUP
TOP_K_IN_GROUP = 2
D_EXPERT = D_MODEL // 2
MOE_BLOCK = 128
EPS = 1e-6
NEG_BIG = -1e30

kernel_name = 'hgrn2_diffattn_hmoe_adaln_step'

F32 = jnp.float32


def rms_norm(x, g):
    xf = x.astype(F32)
    y = xf * lax.rsqrt(jnp.mean(xf * xf, axis=-1, keepdims=True) + EPS)
    return y.astype(x.dtype) * g


def adaln_params(c, w, b):
    mod = jax.nn.silu(c) @ w + b
    return jnp.split(mod, 6, axis=-1)


def modulate(x, g, shift, scale):
    return rms_norm(x, g) * (1 + scale[:, None]) + shift[:, None]


def rel_bias(qpos, kpos, table):
    n = jnp.maximum(qpos[:, None] - kpos[None, :], 0)
    max_exact = NUM_BUCKETS // 2
    large = max_exact + (jnp.log(jnp.maximum(n, max_exact).astype(F32) / max_exact)
                         / math.log(MAX_DISTANCE / max_exact) * (NUM_BUCKETS - max_exact)).astype(jnp.int32)
    bucket = jnp.where(n < max_exact, n, jnp.minimum(large, NUM_BUCKETS - 1))
    return jnp.moveaxis(table[bucket], -1, 0).astype(F32)


def hgrn2_recurrence(q, k, log_f, v, s0):
    B, L, H, DK = q.shape
    DV = v.shape[-1]
    c = min(HG_CHUNK, L)
    n = -(-L // c)
    pad = n * c - L
    if pad:
        padw = ((0, 0), (0, pad), (0, 0), (0, 0))
        q, k, log_f, v = [jnp.pad(a, padw) for a in (q, k, log_f, v)]

    def to_chunks(a):
        return a.reshape(B, n, c, H, a.shape[-1]).transpose(1, 0, 2, 3, 4)

    causal = jnp.tril(jnp.ones((c, c), bool))[None, :, :, None, None]

    def step(S, inp):
        qc, kc, gc, vc = inp
        b = jnp.cumsum(gc, axis=1)
        diff = jnp.where(causal, b[:, :, None] - b[:, None], 0.0)
        decay = jnp.where(causal, jnp.exp(diff), 0.0)
        scores = jnp.einsum('bthd,bshd,btshd->bhts', qc, kc, decay)
        o = (jnp.einsum('bhts,bshv->bthv', scores, vc)
             + jnp.einsum('bthd,bhdv->bthv', qc * jnp.exp(b), S))
        b_last = b[:, -1]
        S = (jnp.exp(b_last)[..., None] * S
             + jnp.einsum('bshd,bshv->bhdv', kc * jnp.exp(b_last[:, None] - b), vc))
        return S, o

    S, o = lax.scan(step, s0, (to_chunks(q), to_chunks(k), to_chunks(log_f), to_chunks(v)))
    o = o.transpose(1, 0, 2, 3, 4).reshape(B, n * c, H, DV)[:, :L]
    return o, S


def hgrn2_mixer(h, s0, w_in, lb, out_norm, w_out):
    B, L, _ = h.shape
    q, f, i, g = jnp.split(h @ w_in, 4, axis=-1)
    ff = f.astype(F32)
    log_f = jnp.log(lb + (1 - lb) * jax.nn.sigmoid(ff))
    k = (1 - lb) * jax.nn.sigmoid(-ff)
    shp = (B, L, HG_HEADS, HG_DK)
    o, s_new = hgrn2_recurrence(jax.nn.silu(q).astype(F32).reshape(shp), k.reshape(shp),
                                log_f.reshape(shp), i.astype(F32).reshape(B, L, HG_HEADS, HG_DV), s0)
    o = rms_norm(o, out_norm.astype(F32)) * jax.nn.silu(g.astype(F32)).reshape(B, L, HG_HEADS, HG_DV)
    return o.astype(h.dtype).reshape(B, L, -1) @ w_out, s_new


def diff_lambda_value(lp, lam_init):
    lpf = lp.astype(F32)
    return jnp.exp(jnp.sum(lpf[0] * lpf[1])) - jnp.exp(jnp.sum(lpf[2] * lpf[3])) + lam_init


def diff_qkv(h, w_in, qk_norm):
    B, L, _ = h.shape
    qkv = h @ w_in
    qw = DF_HEADS * 2 * DF_DH
    q = qkv[..., :qw].reshape(B, L, DF_HEADS, 2, DF_DH)
    k = qkv[..., qw:2 * qw].reshape(B, L, DF_HEADS, 2, DF_DH)
    v = qkv[..., 2 * qw:].reshape(B, L, DF_HEADS, DF_DV)
    return rms_norm(q, qk_norm[0]), rms_norm(k, qk_norm[1]), v


def diff_attention_prompt(q, k, v, lam, pos, table):
    B, L = q.shape[:2]
    qb = min(Q_BLOCK, L)
    nb = L // qb
    scale = DF_DH ** -0.5

    def block(start):
        q_blk = lax.dynamic_slice_in_dim(q, start, qb, axis=1)
        qpos = lax.dynamic_slice_in_dim(pos, start, qb)
        s = jnp.einsum('bqhcd,bkhcd->bhcqk', q_blk, k).astype(F32) * scale
        s = s + rel_bias(qpos, pos, table)[None, :, None]
        s = jnp.where(qpos[:, None] >= pos[None, :], s, NEG_BIG)
        p = jax.nn.softmax(s, axis=-1)
        pd = p[:, :, 0] - lam * p[:, :, 1]
        return jnp.einsum('bhqk,bkhv->bqhv', pd.astype(v.dtype), v)

    o = lax.map(block, jnp.arange(nb) * qb)
    return o.transpose(1, 0, 2, 3, 4).reshape(B, L, DF_HEADS, DF_DV)


def diff_attention_sample(q, k_new, v_new, k_pool, v_pool, page_table, lam, qpos, table):
    Bd, Q = q.shape[:2]
    n_pages = page_table.shape[1]
    scale = DF_DH ** -0.5

    def attend(carry, kb, vb, kpos, mask):
        m, l, acc = carry
        s = jnp.einsum('bqhcd,bkhcd->bhcqk', q, kb).astype(F32) * scale
        s = s + rel_bias(qpos, kpos, table)[None, :, None]
        if mask is not None:
            s = jnp.where(mask, s, NEG_BIG)
        m_new = jnp.maximum(m, s.max(axis=-1))
        alpha = jnp.exp(m - m_new)
        p = jnp.exp(s - m_new[..., None])
        l = l * alpha + p.sum(axis=-1)
        acc = acc * alpha[..., None] + jnp.einsum('bhcqk,bkhv->bhcqv', p, vb.astype(F32))
        return (m_new, l, acc)

    def page_step(carry, inp):
        phys, j = inp
        kb = k_pool[phys].reshape(Bd, PAGE_SIZE, DF_HEADS, 2, DF_DH)
        vb = v_pool[phys]
        kpos = j * PAGE_SIZE + jnp.arange(PAGE_SIZE)
        return attend(carry, kb, vb, kpos, None), None

    init = (jnp.full((Bd, DF_HEADS, 2, Q), NEG_BIG, F32),
            jnp.zeros((Bd, DF_HEADS, 2, Q), F32),
            jnp.zeros((Bd, DF_HEADS, 2, Q, DF_DV), F32))
    carry, _ = lax.scan(page_step, init, (page_table.T, jnp.arange(n_pages)))
    m, l, acc = attend(carry, k_new, v_new, qpos, qpos[:, None] >= qpos[None, :])
    o = acc / l[..., None]
    o = o[:, :, 0] - lam * o[:, :, 1]
    return o.transpose(0, 2, 1, 3).astype(v_new.dtype)


def diff_output(o, subln, w_out, lam_init):
    B, L = o.shape[:2]
    o = rms_norm(o, subln) * (1 - lam_init)
    return o.reshape(B, L, -1) @ w_out


def grouped_expert_mlp(t, expert, gate, w1, w3, w2):
    T, D = t.shape
    K = expert.shape[1]
    E = w1.shape[0]
    A = T * K
    flat_e = expert.reshape(-1)
    order = jnp.argsort(flat_e)
    sorted_e = flat_e[order]
    counts = jnp.bincount(flat_e, length=E)
    padded = (counts + MOE_BLOCK - 1) // MOE_BLOCK * MOE_BLOCK
    start = jnp.cumsum(counts) - counts
    ends_p = jnp.cumsum(padded)
    pstart = ends_p - padded
    dest_sorted = pstart[sorted_e] + jnp.arange(A) - start[sorted_e]
    dest = jnp.zeros((A,), jnp.int32).at[order].set(dest_sorted.astype(jnp.int32))
    n_blocks = (A + E * (MOE_BLOCK - 1) + MOE_BLOCK - 1) // MOE_BLOCK
    slot_tok = jnp.full((n_blocks * MOE_BLOCK,), T, jnp.int32).at[dest].set(
        (jnp.arange(A) // K).astype(jnp.int32))
    t_pad = jnp.concatenate([t, jnp.zeros((1, D), t.dtype)], axis=0)
    xb = t_pad[slot_tok].reshape(n_blocks, MOE_BLOCK, D)
    block_e = jnp.minimum(jnp.searchsorted(ends_p, jnp.arange(n_blocks) * MOE_BLOCK, side='right'), E - 1)

    def run(args):
        xblk, e = args
        hdn = jax.nn.silu(xblk @ w1[e]) * (xblk @ w3[e])
        return hdn @ w2[e]

    yb = lax.map(run, (xb, block_e)).reshape(-1, D)
    y_assign = yb[dest].reshape(T, K, D)
    return jnp.einsum('tk,tkd->td', gate, y_assign)


def hier_moe(h, wg, bg, we, be, w1, w3, w2):
    B, L, D = h.shape
    t = h.reshape(B * L, D)
    T = t.shape[0]
    lg = (t @ wg).astype(F32)
    g_sel = jnp.argmax(lg + bg.astype(F32), axis=-1)
    p_group = jnp.take_along_axis(jax.nn.softmax(lg, axis=-1), g_sel[:, None], axis=1)
    le = (t @ we).astype(F32).reshape(T, N_GROUPS, EXPERTS_PER_GROUP)
    le_sel = jnp.take_along_axis(le, g_sel[:, None, None], axis=1)[:, 0]
    _, top = lax.top_k(le_sel + be.astype(F32)[g_sel], TOP_K_IN_GROUP)
    p_exp = jax.nn.softmax(jnp.take_along_axis(le_sel, top, axis=1), axis=-1)
    gate = (p_group * p_exp).astype(t.dtype)
    expert = (g_sel[:, None] * EXPERTS_PER_GROUP + top).astype(jnp.int32)
    return grouped_expert_mlp(t, expert, gate, w1, w3, w2).reshape(B, L, D)


def setup_inputs(seed: int = 0) -> dict:
    key = jax.random.key(seed)
    ks = iter(jax.random.split(key, 40))

    def nrm(shape, scale):
        return jax.random.normal(next(ks), shape, F32) * scale

    n_pages = PAST_LEN // PAGE_SIZE
    n_used = DEC_BATCH * n_pages
    n_phys = n_used + (n_used + 3) // 4
    D = D_MODEL
    x_prompt = nrm((BATCH, SEQ, D), 1.0)
    x_sample = nrm((DEC_BATCH, DEC_SEQ, D), 1.0)
    c_prompt = nrm((BATCH, D), 1.0)
    c_sample = nrm((DEC_BATCH, D), 1.0)
    cache_k = nrm((N_DIFF_LAYERS, n_phys, PAGE_SIZE, DF_HEADS, 2 * DF_DH), 1.0)
    cache_v = nrm((N_DIFF_LAYERS, n_phys, PAGE_SIZE, DF_HEADS, DF_DV), 1.0)
    state_hgrn = nrm((N_HGRN_LAYERS, DEC_BATCH, HG_HEADS, HG_DK, HG_DV), 0.5)
    page_table = jax.random.permutation(next(ks), n_phys)[:n_used].reshape(DEC_BATCH, n_pages).astype(jnp.int32)
    return {
        'x_prompt': x_prompt, 'x_sample': x_sample, 'c_prompt': c_prompt, 'c_sample': c_sample,
        'cache_k': cache_k, 'cache_v': cache_v, 'state_hgrn': state_hgrn, 'page_table': page_table,
        'norm_mix': 1.0 + nrm((DEPTH, D), 0.02),
        'norm_ffn': 1.0 + nrm((DEPTH, D), 0.02),
        'ada_w': nrm((DEPTH, D, 6 * D), 0.5 * D ** -0.5),
        'ada_b': nrm((DEPTH, 6 * D), 0.02),
        'hgrn_w_in': nrm((N_HGRN_LAYERS, D, 4 * HG_WIDTH), D ** -0.5),
        'hgrn_lower_bound': nrm((N_HGRN_LAYERS, HG_WIDTH), 1.0),
        'hgrn_out_norm': 1.0 + nrm((N_HGRN_LAYERS, HG_DV), 0.02),
        'hgrn_w_out': nrm((N_HGRN_LAYERS, HG_HEADS * HG_DV, D), (HG_HEADS * HG_DV) ** -0.5),
        'diff_w_in': nrm((N_DIFF_LAYERS, D, 4 * DF_HEADS * DF_DH + DF_HEADS * DF_DV), D ** -0.5),
        'diff_qk_norm': 1.0 + nrm((N_DIFF_LAYERS, 2, DF_DH), 0.02),
        'diff_lambda': nrm((N_DIFF_LAYERS, 4, DF_DH), 0.1),
        'diff_subln': 1.0 + nrm((N_DIFF_LAYERS, DF_DV), 0.02),
        'diff_w_out': nrm((N_DIFF_LAYERS, DF_HEADS * DF_DV, D), (DF_HEADS * DF_DV) ** -0.5),
        'rel_bias_table': nrm((NUM_BUCKETS, DF_HEADS), 0.5),
        'router_group_w': nrm((DEPTH, D, N_GROUPS), D ** -0.5),
        'router_group_b': nrm((DEPTH, N_GROUPS), 0.01),
        'router_expert_w': nrm((DEPTH, D, N_GROUPS * EXPERTS_PER_GROUP), D ** -0.5),
        'router_expert_b': nrm((DEPTH, N_GROUPS, EXPERTS_PER_GROUP), 0.01),
        'expert_w1': nrm((DEPTH, N_EXPERTS, D, D_EXPERT), D ** -0.5),
        'expert_w3': nrm((DEPTH, N_EXPERTS, D, D_EXPERT), D ** -0.5),
        'expert_w2': nrm((DEPTH, N_EXPERTS, D_EXPERT, D), D_EXPERT ** -0.5),
    }


def reference(x_prompt, x_sample, c_prompt, c_sample, cache_k, cache_v, state_hgrn, page_table,
              norm_mix, norm_ffn, ada_w, ada_b, hgrn_w_in, hgrn_lower_bound, hgrn_out_norm, hgrn_w_out,
              diff_w_in, diff_qk_norm, diff_lambda, diff_subln, diff_w_out, rel_bias_table,
              router_group_w, router_group_b, router_expert_w, router_expert_b,
              expert_w1, expert_w3, expert_w2):
    Bp, Lp, _ = x_prompt.shape
    past_len = page_table.shape[1] * PAGE_SIZE
    pos_p = jnp.arange(Lp)
    pos_s = past_len + jnp.arange(x_sample.shape[1])
    lb_prob = jax.nn.softmax(hgrn_lower_bound.astype(F32), axis=0)
    lower_bounds = jnp.cumsum(lb_prob, axis=0) - lb_prob[0]

    xp, xs = x_prompt, x_sample
    k_p, v_p, k_s, v_s, st_p, st_s = [], [], [], [], [], []
    for i in range(DEPTH):
        j = i // N_MIXERS
        sh1_p, sc1_p, ga1_p, sh2_p, sc2_p, ga2_p = adaln_params(c_prompt, ada_w[i], ada_b[i])
        sh1_s, sc1_s, ga1_s, sh2_s, sc2_s, ga2_s = adaln_params(c_sample, ada_w[i], ada_b[i])
        hp = modulate(xp, norm_mix[i], sh1_p, sc1_p)
        hs = modulate(xs, norm_mix[i], sh1_s, sc1_s)
        if i % N_MIXERS == 0:
            s0 = jnp.zeros((Bp, HG_HEADS, HG_DK, HG_DV), F32)
            op, sp = hgrn2_mixer(hp, s0, hgrn_w_in[j], lower_bounds[j], hgrn_out_norm[j], hgrn_w_out[j])
            os_, ss = hgrn2_mixer(hs, state_hgrn[j].astype(F32), hgrn_w_in[j], lower_bounds[j],
                                  hgrn_out_norm[j], hgrn_w_out[j])
            st_p.append(sp.astype(state_hgrn.dtype))
            st_s.append(ss.astype(state_hgrn.dtype))
        else:
            lam_init = 0.8 - 0.6 * math.exp(-0.3 * i)
            lam = diff_lambda_value(diff_lambda[j], lam_init)
            qp, kp, vp = diff_qkv(hp, diff_w_in[j], diff_qk_norm[j])
            qs, kss, vss = diff_qkv(hs, diff_w_in[j], diff_qk_norm[j])
            op = diff_attention_prompt(qp, kp, vp, lam, pos_p, rel_bias_table)
            os_ = diff_attention_sample(qs, kss, vss, cache_k[j], cache_v[j], page_table, lam, pos_s,
                                        rel_bias_table)
            op = diff_output(op, diff_subln[j], diff_w_out[j], lam_init)
            os_ = diff_output(os_, diff_subln[j], diff_w_out[j], lam_init)
            k_p.append(kp.reshape(Bp, Lp, DF_HEADS, 2 * DF_DH))
            v_p.append(vp)
            k_s.append(kss.reshape(xs.shape[0], xs.shape[1], DF_HEADS, 2 * DF_DH))
            v_s.append(vss)
        xp = xp + ga1_p[:, None] * op
        xs = xs + ga1_s[:, None] * os_
        hp = modulate(xp, norm_ffn[i], sh2_p, sc2_p)
        hs = modulate(xs, norm_ffn[i], sh2_s, sc2_s)
        moe_w = (router_group_w[i], router_group_b[i], router_expert_w[i], router_expert_b[i],
                 expert_w1[i], expert_w3[i], expert_w2[i])
        xp = xp + ga2_p[:, None] * hier_moe(hp, *moe_w)
        xs = xs + ga2_s[:, None] * hier_moe(hs, *moe_w)

    new_k_prompt = jnp.stack(k_p)
    new_v_prompt = jnp.stack(v_p)
    new_k_sample = jnp.stack(k_s)
    new_v_sample = jnp.stack(v_s)
    new_state_hgrn_prompt = jnp.stack(st_p)
    new_state_hgrn_sample = jnp.stack(st_s)
    return (xp, xs, new_k_prompt, new_v_prompt, new_k_sample, new_v_sample,
            new_state_hgrn_prompt, new_state_hgrn_sample)
```

```python
import functools
import math

import jax
import jax.numpy as jnp
from jax import lax
from jax.experimental import pallas as pl
from jax.experimental.pallas import tpu as pltpu

F32 = jnp.float32
BF16 = jnp.bfloat16

N_MIXERS = 2
HG_HEADS = 16
DF_HEADS = 16
NUM_BUCKETS = 32
MAX_DISTANCE = 128
N_GROUPS = 4
EXPERTS_PER_GROUP = 8
TOP_K_IN_GROUP = 2
EPS = 1e-6
NEG_BIG = -1e30

LANES = 128
SUBLANES = 8
BF16_ROWS = 16
VMEM_LIMIT = 56 * 1024 * 1024

ROW_TILE = 1024
COL_TILE = 512
HG_CHUNK = 64
HG_SUB = 16
ATT_TILE = 256
PAGES_PER_STEP = 2
MOE_BLOCK = 128
MOE_HID_TILE = 512
MOE_CHUNK_BLOCKS = 8

NT_DIMS = (((1,), (1,)), ((), ()))
TN_DIMS = (((0,), (0,)), ((), ()))


def _cparams(sem):
    return pltpu.CompilerParams(dimension_semantics=sem, vmem_limit_bytes=VMEM_LIMIT)


def _silu(x):
    return x * jax.nn.sigmoid(x)


def _norm_mod(x, g, shift, scale):
    y = x * lax.rsqrt(jnp.mean(x * x, axis=-1, keepdims=True) + EPS)
    return (y * g) * (1.0 + scale) + shift


def _split3(x):
    x1 = x.astype(BF16)
    r1 = x - x1.astype(F32)
    x2 = r1.astype(BF16)
    x3 = (r1 - x2.astype(F32)).astype(BF16)
    return x1, x2, x3


def _adaln_kernel(c_ref, w_ref, b_ref, o_ref):
    a = _silu(c_ref[...]).astype(BF16)
    o_ref[...] = jnp.dot(a, w_ref[...].astype(BF16), preferred_element_type=F32) + b_ref[...]


def adaln_all(c_all, ada_w, ada_b, tn=1024):
    depth, d, n = ada_w.shape
    r = c_all.shape[0]
    return pl.pallas_call(
        _adaln_kernel,
        grid=(depth, n // tn),
        in_specs=[pl.BlockSpec((r, d), lambda l, j: (0, 0)),
                  pl.BlockSpec((None, d, tn), lambda l, j: (l, 0, j)),
                  pl.BlockSpec((None, 1, tn), lambda l, j: (l, 0, j))],
        out_specs=pl.BlockSpec((None, r, tn), lambda l, j: (l, 0, j)),
        out_shape=jax.ShapeDtypeStruct((depth, r, n), F32),
        compiler_params=_cparams(("arbitrary", "arbitrary")),
        name="adaln",
    )(c_all, ada_w, ada_b.reshape(depth, 1, n))


def _mod_spec(a, tm, rows_per_seq):
    if a.ndim == 3:
        blocks_per_seq = rows_per_seq // tm
        return pl.BlockSpec((None, 1, a.shape[-1]), lambda i, j: (i // blocks_per_seq, 0, 0))
    return pl.BlockSpec((tm, a.shape[-1]), lambda i, j: (i, 0))


def _mod_spec_cols(a, tm, tn, rows_per_seq):
    if a.ndim == 3:
        blocks_per_seq = rows_per_seq // tm
        return pl.BlockSpec((None, 1, tn), lambda i, j: (i // blocks_per_seq, 0, j))
    return pl.BlockSpec((tm, tn), lambda i, j: (i, j))


def _nmm_plain_kernel(x_ref, g_ref, sh_ref, sc_ref, w_ref, o_ref, h_scr):
    @pl.when(pl.program_id(1) == 0)
    def _():
        h_scr[...] = _norm_mod(x_ref[...], g_ref[...], sh_ref[...], sc_ref[...]).astype(BF16)

    o_ref[...] = jnp.dot(h_scr[...], w_ref[...], preferred_element_type=F32)


def _qk_norm_store(acc, wn, out_ref):
    tn = acc.shape[1]
    lo_mask = lax.broadcasted_iota(jnp.int32, (1, LANES), 1) < (LANES // 2)
    for gi in range(tn // LANES):
        s = acc[:, gi * LANES:(gi + 1) * LANES]
        sq = s * s
        lo = jnp.sum(jnp.where(lo_mask, sq, 0.0), axis=-1, keepdims=True)
        hi = jnp.sum(jnp.where(lo_mask, 0.0, sq), axis=-1, keepdims=True)
        ms = jnp.where(lo_mask, lo, hi) * (2.0 / LANES)
        out_ref[:, gi * LANES:(gi + 1) * LANES] = (s * lax.rsqrt(ms + EPS) * wn).astype(out_ref.dtype)


def _nmm_qkv_kernel(x_ref, g_ref, sh_ref, sc_ref, w_ref, qn_ref, kn_ref, q_ref, k_ref, v_ref, h_scr, *, nt):
    j = pl.program_id(1)

    @pl.when(j == 0)
    def _():
        h_scr[...] = _norm_mod(x_ref[...], g_ref[...], sh_ref[...], sc_ref[...]).astype(BF16)

    acc = jnp.dot(h_scr[...], w_ref[...], preferred_element_type=F32)

    @pl.when(j < nt)
    def _():
        _qk_norm_store(acc, qn_ref[...], q_ref)

    @pl.when((j >= nt) & (j < 2 * nt))
    def _():
        _qk_norm_store(acc, kn_ref[...], k_ref)

    @pl.when(j >= 2 * nt)
    def _():
        v_ref[...] = acc


def norm_mod_matmul(x, g, shift, scale, w, rows_per_seq, tm, tn=COL_TILE):
    m, d = x.shape
    n = w.shape[1]
    return pl.pallas_call(
        _nmm_plain_kernel,
        grid=(m // tm, n // tn),
        in_specs=[pl.BlockSpec((tm, d), lambda i, j: (i, 0)),
                  pl.BlockSpec((1, d), lambda i, j: (0, 0)),
                  _mod_spec(shift, tm, rows_per_seq),
                  _mod_spec(scale, tm, rows_per_seq),
                  pl.BlockSpec((d, tn), lambda i, j: (0, j))],
        out_specs=pl.BlockSpec((tm, tn), lambda i, j: (i, j)),
        out_shape=jax.ShapeDtypeStruct((m, n), F32),
        scratch_shapes=[pltpu.VMEM((tm, d), BF16)],
        compiler_params=_cparams(("arbitrary", "arbitrary")),
        name="norm_proj",
    )(x, g, shift, scale, w)


def norm_mod_qkv(x, g, shift, scale, w, qn, kn, rows_per_seq, tm, tn=COL_TILE):
    m, d = x.shape
    part = w.shape[1] // 3
    nt = part // tn
    clamp = lambda j, lo: jnp.clip(j - lo, 0, nt - 1)
    return pl.pallas_call(
        functools.partial(_nmm_qkv_kernel, nt=nt),
        grid=(m // tm, 3 * nt),
        in_specs=[pl.BlockSpec((tm, d), lambda i, j: (i, 0)),
                  pl.BlockSpec((1, d), lambda i, j: (0, 0)),
                  _mod_spec(shift, tm, rows_per_seq),
                  _mod_spec(scale, tm, rows_per_seq),
                  pl.BlockSpec((d, tn), lambda i, j: (0, j)),
                  pl.BlockSpec((1, LANES), lambda i, j: (0, 0)),
                  pl.BlockSpec((1, LANES), lambda i, j: (0, 0))],
        out_specs=[pl.BlockSpec((tm, tn), lambda i, j: (i, clamp(j, 0))),
                   pl.BlockSpec((tm, tn), lambda i, j: (i, clamp(j, nt))),
                   pl.BlockSpec((tm, tn), lambda i, j: (i, clamp(j, 2 * nt)))],
        out_shape=[jax.ShapeDtypeStruct((m, part), BF16),
                   jax.ShapeDtypeStruct((m, part), F32),
                   jax.ShapeDtypeStruct((m, part), F32)],
        scratch_shapes=[pltpu.VMEM((tm, d), BF16)],
        compiler_params=_cparams(("arbitrary", "arbitrary")),
        name="norm_qkv",
    )(x, g, shift, scale, w, qn, kn)


def _mmres_kernel(o_ref, w_ref, x_ref, ga_ref, out_ref):
    out_ref[...] = x_ref[...] + ga_ref[...] * jnp.dot(o_ref[...], w_ref[...], preferred_element_type=F32)


def matmul_residual(o, w, x, gate, rows_per_seq, tm, tn=COL_TILE):
    m, kdim = o.shape
    n = w.shape[1]
    return pl.pallas_call(
        _mmres_kernel,
        grid=(m // tm, n // tn),
        in_specs=[pl.BlockSpec((tm, kdim), lambda i, j: (i, 0)),
                  pl.BlockSpec((kdim, tn), lambda i, j: (0, j)),
                  pl.BlockSpec((tm, tn), lambda i, j: (i, j)),
                  _mod_spec_cols(gate, tm, tn, rows_per_seq)],
        out_specs=pl.BlockSpec((tm, tn), lambda i, j: (i, j)),
        out_shape=jax.ShapeDtypeStruct((m, n), F32),
        compiler_params=_cparams(("arbitrary", "arbitrary")),
        name="out_proj_residual",
    )(o, w, x, gate)


def _nrouter_kernel(x_ref, g_ref, sh_ref, sc_ref, wh_ref, wl_ref, h_ref, lg_ref):
    h = _norm_mod(x_ref[...], g_ref[...], sh_ref[...], sc_ref[...])
    h1 = h.astype(BF16)
    h_ref[...] = h1
    h2 = (h - h1.astype(F32)).astype(BF16)
    wh = wh_ref[...]
    lg_ref[...] = (jnp.dot(h1, wh, preferred_element_type=F32)
                   + jnp.dot(h1, wl_ref[...], preferred_element_type=F32)
                   + jnp.dot(h2, wh, preferred_element_type=F32))


def norm_mod_router(x, g, shift, scale, wr_hi, wr_lo, rows_per_seq, tm):
    m, d = x.shape
    nr = wr_hi.shape[1]
    ij = lambda f: (lambda i: f(i, 0))
    sh_spec = _mod_spec(shift, tm, rows_per_seq)
    sc_spec = _mod_spec(scale, tm, rows_per_seq)
    return pl.pallas_call(
        _nrouter_kernel,
        grid=(m // tm,),
        in_specs=[pl.BlockSpec((tm, d), lambda i: (i, 0)),
                  pl.BlockSpec((1, d), lambda i: (0, 0)),
                  pl.BlockSpec(sh_spec.block_shape, ij(sh_spec.index_map)),
                  pl.BlockSpec(sc_spec.block_shape, ij(sc_spec.index_map)),
                  pl.BlockSpec((d, nr), lambda i: (0, 0)),
                  pl.BlockSpec((d, nr), lambda i: (0, 0))],
        out_specs=[pl.BlockSpec((tm, d), lambda i: (i, 0)),
                   pl.BlockSpec((tm, nr), lambda i: (i, 0))],
        out_shape=[jax.ShapeDtypeStruct((m, d), BF16),
                   jax.ShapeDtypeStruct((m, nr), F32)],
        compiler_params=_cparams(("arbitrary",)),
        name="norm_router",
    )(x, g, shift, scale, wr_hi, wr_lo)


def _chunk_cumsum(x, tri):
    c = x.shape[0]
    if tri is None:
        row = lax.broadcasted_iota(jnp.int32, x.shape, 0)
        out = jnp.zeros_like(x)
        for s in range(c):
            out = out + jnp.where(row >= s, x[s:s + 1], 0.0)
        return out
    x1, x2, x3 = _split3(x)
    return (jnp.dot(tri, x1, preferred_element_type=F32)
            + jnp.dot(tri, x2, preferred_element_type=F32)
            + jnp.dot(tri, x3, preferred_element_type=F32))


def _hgrn_kernel(q_ref, f_ref, i_ref, g_ref, lb_ref, on_ref, s0_ref, o_ref, sn_ref, st_scr, *, seq, chunk, sub):
    lb = lb_ref[...]
    on = on_ref[...]
    st_scr[...] = s0_ref[...].T
    nsub = chunk // sub
    row_sub = lax.broadcasted_iota(jnp.int32, (sub, LANES), 0)
    if chunk > BF16_ROWS:
        tri = (lax.broadcasted_iota(jnp.int32, (chunk, chunk), 0)
               >= lax.broadcasted_iota(jnp.int32, (chunk, chunk), 1)).astype(BF16)
    else:
        tri = None

    def chunk_body(ci, carry):
        r0 = pl.multiple_of(ci * chunk, chunk)
        q = q_ref[pl.ds(r0, chunk), :]
        f = f_ref[pl.ds(r0, chunk), :]
        v = i_ref[pl.ds(r0, chunk), :]
        g = g_ref[pl.ds(r0, chunk), :]
        qs = _silu(q)
        log_f = jnp.log(lb + (1.0 - lb) * jax.nn.sigmoid(f))
        k = (1.0 - lb) * jax.nn.sigmoid(-f)
        b = _chunk_cumsum(log_f, tri)
        st = st_scr[...]
        vb = v.astype(BF16)
        o_inter = lax.dot_general((qs * jnp.exp(b)).astype(BF16), st.astype(BF16), NT_DIMS,
                                  preferred_element_type=F32)
        parts = []
        for blk in range(nsub):
            lo = blk * sub
            q_b, b_b, k_b, v_b = qs[lo:lo + sub], b[lo:lo + sub], k[lo:lo + sub], v[lo:lo + sub]
            o_b = o_inter[lo:lo + sub]
            if blk > 0:
                b_ref_row = b[lo - 1:lo]
                qd = (q_b * jnp.exp(b_b - b_ref_row)).astype(BF16)
                kd = (k[:lo] * jnp.exp(b_ref_row - b[:lo])).astype(BF16)
                sc = lax.dot_general(qd, kd, NT_DIMS, preferred_element_type=F32)
                o_b = o_b + jnp.dot(sc.astype(BF16), vb[:lo], preferred_element_type=F32)
            for s in range(sub):
                e = jnp.exp(b_b - b_b[s:s + 1])
                a = jnp.where(row_sub >= s, q_b * e * k_b[s:s + 1], 0.0)
                o_b = o_b + jnp.sum(a, axis=-1, keepdims=True) * v_b[s:s + 1]
            parts.append(o_b)
        o = parts[0] if nsub == 1 else jnp.concatenate(parts, axis=0)
        b_last = b[chunk - 1:chunk]
        kdc = (k * jnp.exp(b_last - b)).astype(BF16)
        upd = lax.dot_general(vb, kdc, TN_DIMS, preferred_element_type=F32)
        st_scr[...] = st * jnp.exp(b_last) + upd
        y = o * lax.rsqrt(jnp.mean(o * o, axis=-1, keepdims=True) + EPS) * on * _silu(g)
        o_ref[pl.ds(r0, chunk), :] = y.astype(o_ref.dtype)
        return carry

    lax.fori_loop(0, seq // chunk, chunk_body, 0)
    sn_ref[...] = st_scr[...].T


def hgrn_recurrence(qfig, lb, out_norm, s0, batch, seq, out_dtype):
    heads = s0.shape[1]
    dk = s0.shape[2]
    chunk = min(HG_CHUNK, seq)
    sub = min(HG_SUB, chunk)
    col = lambda part: (lambda b, h: (b, part * heads + h))
    return pl.pallas_call(
        functools.partial(_hgrn_kernel, seq=seq, chunk=chunk, sub=sub),
        grid=(batch, heads),
        in_specs=[pl.BlockSpec((seq, dk), col(0)),
                  pl.BlockSpec((seq, dk), col(1)),
                  pl.BlockSpec((seq, dk), col(2)),
                  pl.BlockSpec((seq, dk), col(3)),
                  pl.BlockSpec((1, dk), lambda b, h: (0, h)),
                  pl.BlockSpec((1, dk), lambda b, h: (0, 0)),
                  pl.BlockSpec((None, None, dk, dk), lambda b, h: (b, h, 0, 0))],
        out_specs=[pl.BlockSpec((seq, dk), lambda b, h: (b, h)),
                   pl.BlockSpec((None, None, dk, dk), lambda b, h: (b, h, 0, 0))],
        out_shape=[jax.ShapeDtypeStruct((batch * seq, heads * dk), out_dtype),
                   jax.ShapeDtypeStruct(s0.shape, F32)],
        scratch_shapes=[pltpu.VMEM((dk, dk), F32)],
        compiler_params=_cparams(("arbitrary", "arbitrary")),
        name="hgrn_recurrence",
    )(qfig, qfig, qfig, qfig, lb, out_norm, s0)


def _bucket(n):
    max_exact = NUM_BUCKETS // 2
    large = max_exact + (jnp.log(jnp.maximum(n, max_exact).astype(F32) / max_exact)
                         / math.log(MAX_DISTANCE / max_exact) * (NUM_BUCKETS - max_exact)).astype(jnp.int32)
    return jnp.where(n < max_exact, n, jnp.minimum(large, NUM_BUCKETS - 1))


def _bias_tile(qpos, kpos, table):
    n = jnp.maximum(qpos[:, None] - kpos[None, :], 0)
    bias = jnp.moveaxis(table[_bucket(n)], -1, 0).astype(F32)
    return jnp.where((qpos[:, None] >= kpos[None, :])[None], bias, NEG_BIG)


def _attn_prompt_kernel(lam_ref, bc_ref, q_ref, k_ref, v_ref, bd_ref, bs_ref, sub_ref, o_ref, kb, vb,
                        *, tile, out_scale):
    h = pl.program_id(1)
    qi = pl.program_id(2)

    @pl.when(qi == 0)
    def _():
        kb[...] = k_ref[...].astype(BF16)
        vb[...] = v_ref[...].astype(BF16)

    q = q_ref[...]
    lo_mask = lax.broadcasted_iota(jnp.int32, q.shape, 1) < (LANES // 2)
    zero = jnp.zeros_like(q)
    qq = jnp.concatenate([jnp.where(lo_mask, q, zero), jnp.where(lo_mask, zero, q)], axis=0)
    bias_far = bc_ref[h]

    def step(kt, vt, bias, carry):
        m, l, acc = carry
        s = lax.dot_general(qq, kt, NT_DIMS, preferred_element_type=F32) + bias
        m_new = jnp.maximum(m, jnp.max(s, axis=-1, keepdims=True))
        alpha = jnp.exp(m - m_new)
        p = jnp.exp(s - m_new)
        l = l * alpha + jnp.sum(p, axis=-1, keepdims=True)
        acc = acc * alpha + jnp.dot(p.astype(BF16), vt, preferred_element_type=F32)
        return m_new, l, acc

    def body(kv, carry):
        r0 = pl.multiple_of(kv * tile, tile)
        bias = jnp.where(kv == qi - 1, bs_ref[...], bias_far)
        return step(kb[pl.ds(r0, tile), :], vb[pl.ds(r0, tile), :], bias, carry)

    init = (jnp.full((2 * tile, 1), NEG_BIG, F32), jnp.zeros((2 * tile, 1), F32),
            jnp.zeros((2 * tile, LANES), F32))
    carry = lax.fori_loop(0, qi, body, init)
    d0 = pl.multiple_of(qi * tile, tile)
    m, l, acc = step(kb[pl.ds(d0, tile), :], vb[pl.ds(d0, tile), :], bd_ref[...], carry)
    o = acc / l
    od = o[:tile] - lam_ref[0] * o[tile:]
    y = od * lax.rsqrt(jnp.mean(od * od, axis=-1, keepdims=True) + EPS) * sub_ref[...] * out_scale
    o_ref[...] = y.astype(o_ref.dtype)


def attn_prompt(q, k, v, lam, bias_far, bias_diag, bias_sub, subln, batch, seq, out_scale):
    heads = q.shape[1] // LANES
    tile = min(ATT_TILE, seq)
    nq = seq // tile
    smem = pl.BlockSpec(memory_space=pltpu.SMEM)
    return pl.pallas_call(
        functools.partial(_attn_prompt_kernel, tile=tile, out_scale=out_scale),
        grid=(batch, heads, nq),
        in_specs=[smem, smem,
                  pl.BlockSpec((tile, LANES), lambda b, h, i: (b * nq + i, h)),
                  pl.BlockSpec((seq, LANES), lambda b, h, i: (b, h)),
                  pl.BlockSpec((seq, LANES), lambda b, h, i: (b, h)),
                  pl.BlockSpec((None, 2 * tile, tile), lambda b, h, i: (h, 0, 0)),
                  pl.BlockSpec((None, 2 * tile, tile), lambda b, h, i: (h, 0, 0)),
                  pl.BlockSpec((1, LANES), lambda b, h, i: (0, 0))],
        out_specs=pl.BlockSpec((tile, LANES), lambda b, h, i: (b * nq + i, h)),
        out_shape=jax.ShapeDtypeStruct(q.shape, BF16),
        scratch_shapes=[pltpu.VMEM((seq, LANES), BF16), pltpu.VMEM((seq, LANES), BF16)],
        compiler_params=_cparams(("arbitrary", "arbitrary", "arbitrary")),
        name="attn_prompt",
    )(lam, bias_far, q, k, v, bias_diag, bias_sub, subln)


def _attn_decode_kernel(pt_ref, lam_ref, q_ref, *refs, npg, heads, n_pages, out_scale):
    k_refs = refs[:npg]
    v_refs = refs[npg:2 * npg]
    kn_ref, vn_ref, bc_ref, bl_ref, bn_ref, sub_ref, o_ref, m_scr, l_scr, acc_scr = refs[2 * npg:]
    p = pl.program_id(1)
    rows = q_ref.shape[0] // heads

    @pl.when(p == 0)
    def _():
        m_scr[...] = jnp.full(m_scr.shape, NEG_BIG, F32)
        l_scr[...] = jnp.zeros(l_scr.shape, F32)
        acc_scr[...] = jnp.zeros(acc_scr.shape, F32)

    q = q_ref[...]

    def page(k_ref, v_ref, bias):
        s = jnp.concatenate(
            [lax.dot_general(q[h * rows:(h + 1) * rows], k_ref[:, h, :].astype(BF16), NT_DIMS,
                             preferred_element_type=F32) for h in range(heads)], axis=0) + bias
        m_old = m_scr[...]
        m_new = jnp.maximum(m_old, jnp.max(s, axis=-1, keepdims=True))
        alpha = jnp.exp(m_old - m_new)
        pe = jnp.exp(s - m_new)
        l_scr[...] = l_scr[...] * alpha + jnp.sum(pe, axis=-1, keepdims=True)
        pb = pe.astype(BF16)
        pv = jnp.concatenate(
            [jnp.dot(pb[h * rows:(h + 1) * rows], v_ref[:, h, :].astype(BF16), preferred_element_type=F32)
             for h in range(heads)], axis=0)
        acc_scr[...] = acc_scr[...] * alpha + pv
        m_scr[...] = m_new

    for gi in range(npg):
        bias = jnp.where(p * npg + gi == n_pages - 1, bl_ref[...], bc_ref[...])
        page(k_refs[gi], v_refs[gi], bias)

    @pl.when(p == pl.num_programs(1) - 1)
    def _():
        page(kn_ref, vn_ref, bn_ref[...])
        o = acc_scr[...] / l_scr[...]
        o4 = o.reshape(heads, 2, rows // 2, LANES)
        od = o4[:, 0] - lam_ref[0] * o4[:, 1]
        y = od * lax.rsqrt(jnp.mean(od * od, axis=-1, keepdims=True) + EPS) * sub_ref[...] * out_scale
        o_ref[...] = y


def attn_decode(qz, cache_k, cache_v, layer, k_new, v_new, page_table, lam, bias_far, bias_last, bias_new,
                subln, out_scale):
    bd, qrows, _ = qz.shape
    _, _, page, heads, dh = cache_k.shape
    n_pages = page_table.shape[1]
    npg = PAGES_PER_STEP
    nq = qrows // heads // 2
    page_spec = lambda gi: pl.BlockSpec(
        (None, None, page, heads, dh), lambda b, p, pt: (layer, pt[b, p * npg + gi], 0, 0, 0))
    const2 = lambda b, p, pt: (0, 0)
    new_spec = pl.BlockSpec((None, page, heads, dh), lambda b, p, pt: (b, 0, 0, 0))
    grid_spec = pltpu.PrefetchScalarGridSpec(
        num_scalar_prefetch=1,
        grid=(bd, n_pages // npg),
        in_specs=[pl.BlockSpec(memory_space=pltpu.SMEM),
                  pl.BlockSpec((None, qrows, dh), lambda b, p, pt: (b, 0, 0))]
                 + [page_spec(gi) for gi in range(npg)] + [page_spec(gi) for gi in range(npg)]
                 + [new_spec, new_spec,
                    pl.BlockSpec((qrows, dh), const2), pl.BlockSpec((qrows, dh), const2),
                    pl.BlockSpec((qrows, dh), const2), pl.BlockSpec((1, dh), const2)],
        out_specs=pl.BlockSpec((None, heads, nq, dh), lambda b, p, pt: (b, 0, 0, 0)),
        scratch_shapes=[pltpu.VMEM((qrows, 1), F32), pltpu.VMEM((qrows, 1), F32), pltpu.VMEM((qrows, dh), F32)],
    )
    return pl.pallas_call(
        functools.partial(_attn_decode_kernel, npg=npg, heads=heads, n_pages=n_pages, out_scale=out_scale),
        grid_spec=grid_spec,
        out_shape=jax.ShapeDtypeStruct((bd, heads, nq, dh), F32),
        compiler_params=_cparams(("arbitrary", "arbitrary")),
        name="attn_decode",
    )(page_table, lam, qz, *([cache_k] * npg), *([cache_v] * npg), k_new, v_new,
      bias_far, bias_last, bias_new, subln)


def route(logits, bg, be):
    t = logits.shape[0]
    lg = logits[:, :N_GROUPS]
    le = logits[:, N_GROUPS:N_GROUPS * (1 + EXPERTS_PER_GROUP)].reshape(t, N_GROUPS, EXPERTS_PER_GROUP)
    g_sel = jnp.argmax(lg + bg.astype(F32), axis=-1)
    p_group = jnp.take_along_axis(jax.nn.softmax(lg, axis=-1), g_sel[:, None], axis=1)
    le_sel = jnp.take_along_axis(le, g_sel[:, None, None], axis=1)[:, 0]
    _, top = lax.top_k(le_sel + be.astype(F32)[g_sel], TOP_K_IN_GROUP)
    p_exp = jax.nn.softmax(jnp.take_along_axis(le_sel, top, axis=1), axis=-1)
    gate = p_group * p_exp
    expert = (g_sel[:, None] * EXPERTS_PER_GROUP + top).astype(jnp.int32)
    return expert, gate


def moe_plan(expert, n_experts, n_hid_tiles):
    t, k = expert.shape
    a = t * k
    bm = MOE_BLOCK
    flat_e = expert.reshape(-1)
    order = jnp.argsort(flat_e)
    sorted_e = flat_e[order]
    counts = jnp.bincount(flat_e, length=n_experts)
    padded = (counts + bm - 1) // bm * bm
    start = jnp.cumsum(counts) - counts
    ends_p = jnp.cumsum(padded)
    pstart = ends_p - padded
    dest_sorted = pstart[sorted_e] + jnp.arange(a) - start[sorted_e]
    dest = jnp.zeros((a,), jnp.int32).at[order].set(dest_sorted.astype(jnp.int32))
    n_blocks = (a + n_experts * (bm - 1) + bm - 1) // bm
    slot_tok = jnp.zeros((n_blocks * bm,), jnp.int32).at[dest].set((jnp.arange(a) // k).astype(jnp.int32))

    bidx = jnp.arange(n_blocks, dtype=jnp.int32)
    block_e = jnp.minimum(jnp.searchsorted(ends_p, bidx * bm, side='right'), n_experts - 1).astype(jnp.int32)
    used = (ends_p[-1] // bm).astype(jnp.int32)
    first_blk = (pstart // bm).astype(jnp.int32)
    nblk_e = (padded // bm).astype(jnp.int32)
    bi = bidx - first_blk[block_e]
    r = bi % MOE_CHUNK_BLOCKS
    cb = bidx - r
    nbc = jnp.minimum(MOE_CHUNK_BLOCKS, nblk_e[block_e] - (bi - r))
    valid_b = bidx < used
    ns = n_hid_tiles * n_blocks
    cols = []
    for ti in range(n_hid_tiles):
        step = jnp.where(valid_b, n_hid_tiles * cb + ti * nbc + r, ns)
        last = ti == n_hid_tiles - 1
        flags = 1 + (2 if ti == 0 else 0) + (4 if last else 0) + jnp.where(r == 0, 8, 0)
        vals = jnp.stack([block_e, jnp.full_like(bidx, ti), bidx, r, bidx if last else cb,
                          flags.astype(jnp.int32)])
        cols.append((step, vals))
    steps = jnp.concatenate([c[0] for c in cols])
    vals = jnp.concatenate([c[1] for c in cols], axis=1)
    table = jnp.zeros((6, ns), jnp.int32).at[:, steps].set(vals, mode='drop')
    n_valid = n_hid_tiles * used
    sidx = jnp.arange(ns)
    tail = table[:, jnp.maximum(n_valid - 1, 0)]
    tail = tail.at[5].set(0)
    table = jnp.where(sidx[None, :] < n_valid, table, tail[:, None])
    return dest, slot_tok, table


def _moe_kernel(se, st, sxb, sr, sob, sfl, x_ref, w1_ref, w3_ref, w2_ref, o_ref, w1b, w3b, w2b, acc):
    s = pl.program_id(0)
    fl = sfl[s]
    valid = (fl & 1) != 0
    first = (fl & 2) != 0
    last = (fl & 4) != 0
    load_w = (fl & 8) != 0

    @pl.when(valid & load_w)
    def _():
        w1b[...] = w1_ref[...].astype(BF16)
        w3b[...] = w3_ref[...].astype(BF16)
        w2b[...] = w2_ref[...].astype(BF16)

    @pl.when(valid)
    def _():
        x = x_ref[...]
        a = jnp.dot(x, w1b[...], preferred_element_type=F32)
        b = jnp.dot(x, w3b[...], preferred_element_type=F32)
        hdn = (_silu(a) * b).astype(BF16)
        y = jnp.dot(hdn, w2b[...], preferred_element_type=F32)
        r = sr[s]

        @pl.when(first & last)
        def _():
            o_ref[...] = y

        @pl.when(first & jnp.logical_not(last))
        def _():
            acc[r] = y

        @pl.when(jnp.logical_not(first) & jnp.logical_not(last))
        def _():
            acc[r] = acc[r] + y

        @pl.when(jnp.logical_not(first) & last)
        def _():
            o_ref[...] = acc[r] + y


def moe_experts(xs, table, w1, w3, w2, layer):
    np_, d = xs.shape
    de = w1.shape[-1]
    th = MOE_HID_TILE
    ns = table.shape[1]
    bm = MOE_BLOCK
    grid_spec = pltpu.PrefetchScalarGridSpec(
        num_scalar_prefetch=6,
        grid=(ns,),
        in_specs=[pl.BlockSpec((bm, d), lambda s, se, st, sxb, sr, sob, sfl: (sxb[s], 0)),
                  pl.BlockSpec((None, None, d, th), lambda s, se, st, sxb, sr, sob, sfl: (layer, se[s], 0, st[s])),
                  pl.BlockSpec((None, None, d, th), lambda s, se, st, sxb, sr, sob, sfl: (layer, se[s], 0, st[s])),
                  pl.BlockSpec((None, None, th, d), lambda s, se, st, sxb, sr, sob, sfl: (layer, se[s], st[s], 0))],
        out_specs=pl.BlockSpec((bm, d), lambda s, se, st, sxb, sr, sob, sfl: (sob[s], 0)),
        scratch_shapes=[pltpu.VMEM((d, th), BF16), pltpu.VMEM((d, th), BF16), pltpu.VMEM((th, d), BF16),
                        pltpu.VMEM((MOE_CHUNK_BLOCKS, bm, d), F32)],
    )
    return pl.pallas_call(
        _moe_kernel,
        grid_spec=grid_spec,
        out_shape=jax.ShapeDtypeStruct((np_, d), F32),
        compiler_params=_cparams(("arbitrary",)),
        name="moe_experts",
    )(table[0], table[1], table[2], table[3], table[4], table[5], xs, w1, w3, w2)


def _rows(a, reps):
    return jnp.repeat(a, reps, axis=0)


def kernel(x_prompt, x_sample, c_prompt, c_sample, cache_k, cache_v, state_hgrn, page_table, norm_mix, norm_ffn, ada_w, ada_b, hgrn_w_in, hgrn_lower_bound, hgrn_out_norm, hgrn_w_out, diff_w_in, diff_qk_norm, diff_lambda, diff_subln, diff_w_out, rel_bias_table, router_group_w, router_group_b, router_expert_w, router_expert_b, expert_w1, expert_w3, expert_w2):
    bp, lp, d = x_prompt.shape
    bs, ls, _ = x_sample.shape
    mp, ms = bp * lp, bs * ls
    depth = ada_w.shape[0]
    page = cache_k.shape[2]
    n_pages = page_table.shape[1]
    past_len = n_pages * page
    n_experts = expert_w1.shape[1]
    assert lp % ROW_TILE == 0 and ms % BF16_ROWS == 0
    assert MAX_DISTANCE <= min(ATT_TILE, lp) and MAX_DISTANCE <= page and ls <= page
    assert expert_w1.shape[-1] % MOE_HID_TILE == 0 and n_pages % PAGES_PER_STEP == 0

    xp = x_prompt.reshape(mp, d)
    xs = x_sample.reshape(ms, d)

    n_c = bp + bs
    pad = (-n_c) % BF16_ROWS
    c_all = jnp.concatenate([c_prompt, c_sample, jnp.zeros((pad, d), F32)], axis=0)
    mod = adaln_all(c_all, ada_w, ada_b)

    lb_prob = jax.nn.softmax(hgrn_lower_bound.astype(F32), axis=0)
    lower_bounds = jnp.cumsum(lb_prob, axis=0) - lb_prob[0]

    tile = min(ATT_TILE, lp)
    ar = jnp.arange(tile)
    b_diag = _bias_tile(ar, ar, rel_bias_table)
    b_sub = _bias_tile(ar + tile, ar, rel_bias_table)
    bias_diag = jnp.concatenate([b_diag, b_diag], axis=1)
    bias_sub = jnp.concatenate([b_sub, b_sub], axis=1)
    bias_far = rel_bias_table[_bucket(jnp.array(MAX_DISTANCE, jnp.int32))].astype(F32)
    pos_s = past_len + jnp.arange(ls)

    def dec_rows(b_hqk):
        hq = jnp.stack([b_hqk, b_hqk], axis=1)
        return hq.reshape(DF_HEADS * 2 * ls, b_hqk.shape[-1])

    dbias_far = jnp.broadcast_to(jnp.repeat(bias_far, 2 * ls)[:, None], (DF_HEADS * 2 * ls, page))
    dbias_last = dec_rows(_bias_tile(pos_s, past_len - page + jnp.arange(page), rel_bias_table))
    new_kpos = jnp.where(jnp.arange(page) < ls, past_len + jnp.arange(page), past_len + 2 * page)
    dbias_new = dec_rows(_bias_tile(pos_s, new_kpos, rel_bias_table))

    outs = {k: [] for k in ('k_p', 'v_p', 'k_s', 'v_s', 'st_p', 'st_s')}
    for i in range(depth):
        j = i // N_MIXERS
        mods = jnp.split(mod[i], 6, axis=-1)
        mp_ = [m[:bp].reshape(bp, 1, d) for m in mods]
        ms_ = [_rows(m[bp:bp + bs], ls) for m in mods]
        g_mix = norm_mix[i].reshape(1, d)
        g_ffn = norm_ffn[i].reshape(1, d)

        if i % N_MIXERS == 0:
            w_in = hgrn_w_in[j].astype(BF16)
            w_out = hgrn_w_out[j].astype(BF16)
            lb = lower_bounds[j].reshape(1, -1)
            onorm = hgrn_out_norm[j].astype(F32).reshape(1, -1)
            dk = onorm.shape[1]
            qfig_p = norm_mod_matmul(xp, g_mix, mp_[0], mp_[1], w_in, lp, ROW_TILE)
            qfig_s = norm_mod_matmul(xs, g_mix, ms_[0], ms_[1], w_in, ls, ms)
            s0_p = jnp.zeros((bp, HG_HEADS, dk, dk), F32)
            op, sp = hgrn_recurrence(qfig_p, lb, onorm, s0_p, bp, lp, BF16)
            os_, ss = hgrn_recurrence(qfig_s, lb, onorm, state_hgrn[j].astype(F32), bs, ls, F32)
            os_ = os_.astype(BF16)
            outs['st_p'].append(sp.astype(state_hgrn.dtype))
            outs['st_s'].append(ss.astype(state_hgrn.dtype))
        else:
            lam_init = 0.8 - 0.6 * math.exp(-0.3 * i)
            lpf = diff_lambda[j].astype(F32)
            lam = (jnp.exp(jnp.sum(lpf[0] * lpf[1])) - jnp.exp(jnp.sum(lpf[2] * lpf[3])) + lam_init).reshape(1)
            w_in = diff_w_in[j].astype(BF16)
            w_out = diff_w_out[j].astype(BF16)
            dh = diff_qk_norm.shape[-1]
            qn = jnp.tile(diff_qk_norm[j, 0], 2).reshape(1, 2 * dh) * (dh ** -0.5)
            kn = jnp.tile(diff_qk_norm[j, 1], 2).reshape(1, 2 * dh)
            subln = diff_subln[j].reshape(1, -1)
            q_p, k_p, v_p = norm_mod_qkv(xp, g_mix, mp_[0], mp_[1], w_in, qn, kn, lp, ROW_TILE)
            q_s, k_s, v_s = norm_mod_qkv(xs, g_mix, ms_[0], ms_[1], w_in, qn, kn, ls, ms)
            op = attn_prompt(q_p, k_p, v_p, lam, bias_far, bias_diag, bias_sub, subln, bp, lp, 1.0 - lam_init)
            q4 = q_s.reshape(bs, ls, DF_HEADS, 2 * dh).transpose(0, 2, 1, 3)
            lo = jnp.arange(2 * dh) < dh
            qz = jnp.stack([jnp.where(lo, q4, 0), jnp.where(lo, 0, q4)], axis=2).reshape(bs, DF_HEADS * 2 * ls, 2 * dh)
            k_s5 = k_s.reshape(bs, ls, DF_HEADS, 2 * dh)
            v_s5 = v_s.reshape(bs, ls, DF_HEADS, 2 * dh)
            padw = ((0, 0), (0, page - ls), (0, 0), (0, 0))
            o_s = attn_decode(qz, cache_k, cache_v, j, jnp.pad(k_s5, padw), jnp.pad(v_s5, padw), page_table, lam,
                              dbias_far, dbias_last, dbias_new, subln, 1.0 - lam_init)
            os_ = o_s.transpose(0, 2, 1, 3).reshape(ms, DF_HEADS * 2 * dh).astype(BF16)
            outs['k_p'].append(k_p.reshape(bp, lp, DF_HEADS, 2 * dh))
            outs['v_p'].append(v_p.reshape(bp, lp, DF_HEADS, 2 * dh))
            outs['k_s'].append(k_s5)
            outs['v_s'].append(v_s5)

        xp = matmul_residual(op, w_out, xp, mp_[2], lp, ROW_TILE)
        xs = matmul_residual(os_, w_out, xs, ms_[2], ls, ms)

        nr = LANES
        wr = jnp.concatenate([router_group_w[i], router_expert_w[i],
                              jnp.zeros((d, nr - N_GROUPS * (1 + EXPERTS_PER_GROUP)), F32)], axis=1)
        wr_hi = wr.astype(BF16)
        wr_lo = (wr - wr_hi.astype(F32)).astype(BF16)
        h_p, lg_p = norm_mod_router(xp, g_ffn, mp_[3], mp_[4], wr_hi, wr_lo, lp, ROW_TILE)
        h_s, lg_s = norm_mod_router(xs, g_ffn, ms_[3], ms_[4], wr_hi, wr_lo, ls, ms)
        h_all = jnp.concatenate([h_p, h_s], axis=0)
        expert, gate = route(jnp.concatenate([lg_p, lg_s], axis=0), router_group_b[i], router_expert_b[i])
        n_hid_tiles = expert_w1.shape[-1] // MOE_HID_TILE
        dest, slot_tok, table = moe_plan(expert, n_experts, n_hid_tiles)
        yb = moe_experts(h_all[slot_tok], table, expert_w1, expert_w3, expert_w2, i)
        y_assign = yb[dest].reshape(mp + ms, TOP_K_IN_GROUP, d)
        y = jnp.sum(gate[:, :, None] * y_assign, axis=1)
        xp = xp + (mp_[5] * y[:mp].reshape(bp, lp, d)).reshape(mp, d)
        xs = xs + ms_[5] * y[mp:]

    return (xp.reshape(bp, lp, d), xs.reshape(bs, ls, d),
            jnp.stack(outs['k_p']), jnp.stack(outs['v_p']), jnp.stack(outs['k_s']), jnp.stack(outs['v_s']),
            jnp.stack(outs['st_p']), jnp.stack(outs['st_s']))
```

```python
import functools
import math

import jax
import jax.numpy as jnp
from jax import lax
from jax.experimental import pallas as pl
from jax.experimental.pallas import tpu as pltpu

F32 = jnp.float32
BF16 = jnp.bfloat16

N_MIXERS = 2
HG_HEADS = 16
DF_HEADS = 16
NUM_BUCKETS = 32
MAX_DISTANCE = 128
N_GROUPS = 4
EXPERTS_PER_GROUP = 8
TOP_K_IN_GROUP = 2
EPS = 1e-6
NEG_BIG = -1e30

LANES = 128
SUBLANES = 8
BF16_ROWS = 16
VMEM_LIMIT = 56 * 1024 * 1024

ROW_TILE = 1024
COL_TILE = 512
HG_CHUNK = 64
HG_SUB = 16
HG_HEADS_PER_STEP = 4
ATT_HEADS_PER_STEP = 2
ATT_TILE = 256
PAGES_PER_STEP = 4
MOE_BLOCK = 128
MOE_HID_TILE = 512
MOE_CHUNK_BLOCKS = 8

NT_DIMS = (((1,), (1,)), ((), ()))
TN_DIMS = (((0,), (0,)), ((), ()))


def _cparams(sem):
    return pltpu.CompilerParams(dimension_semantics=sem, vmem_limit_bytes=VMEM_LIMIT)


def _silu(x):
    return x * jax.nn.sigmoid(x)


def _norm_mod(x, g, shift, scale):
    y = x * lax.rsqrt(jnp.mean(x * x, axis=-1, keepdims=True) + EPS)
    return (y * g) * (1.0 + scale) + shift


def _split3(x):
    x1 = x.astype(BF16)
    r1 = x - x1.astype(F32)
    x2 = r1.astype(BF16)
    x3 = (r1 - x2.astype(F32)).astype(BF16)
    return x1, x2, x3


def _adaln_kernel(c_ref, w_ref, b_ref, o_ref):
    a = _silu(c_ref[...]).astype(BF16)
    o_ref[...] = jnp.dot(a, w_ref[...].astype(BF16), preferred_element_type=F32) + b_ref[...]


def adaln_all(c_all, ada_w, ada_b, tn=1024):
    depth, d, n = ada_w.shape
    r = c_all.shape[0]
    return pl.pallas_call(
        _adaln_kernel,
        grid=(depth, n // tn),
        in_specs=[pl.BlockSpec((r, d), lambda l, j: (0, 0)),
                  pl.BlockSpec((None, d, tn), lambda l, j: (l, 0, j)),
                  pl.BlockSpec((None, 1, tn), lambda l, j: (l, 0, j))],
        out_specs=pl.BlockSpec((None, r, tn), lambda l, j: (l, 0, j)),
        out_shape=jax.ShapeDtypeStruct((depth, r, n), F32),
        compiler_params=_cparams(("arbitrary", "arbitrary")),
        name="adaln",
    )(c_all, ada_w, ada_b.reshape(depth, 1, n))


def _mod_spec(a, tm, rows_per_seq):
    if a.ndim == 3:
        blocks_per_seq = rows_per_seq // tm
        return pl.BlockSpec((None, 1, a.shape[-1]), lambda i, j: (i // blocks_per_seq, 0, 0))
    return pl.BlockSpec((tm, a.shape[-1]), lambda i, j: (i, 0))


def _mod_spec_cols(a, tm, tn, rows_per_seq):
    if a.ndim == 3:
        blocks_per_seq = rows_per_seq // tm
        return pl.BlockSpec((None, 1, tn), lambda i, j: (i // blocks_per_seq, 0, j))
    return pl.BlockSpec((tm, tn), lambda i, j: (i, j))


def _nmm_plain_kernel(x_ref, g_ref, sh_ref, sc_ref, w_ref, o_ref, h_scr):
    @pl.when(pl.program_id(1) == 0)
    def _():
        h_scr[...] = _norm_mod(x_ref[...], g_ref[...], sh_ref[...], sc_ref[...]).astype(BF16)

    o_ref[...] = jnp.dot(h_scr[...], w_ref[...], preferred_element_type=F32)


def _qk_norm_store(acc, wn, out_ref):
    tn = acc.shape[1]
    lo_mask = lax.broadcasted_iota(jnp.int32, (1, LANES), 1) < (LANES // 2)
    for gi in range(tn // LANES):
        s = acc[:, gi * LANES:(gi + 1) * LANES]
        sq = s * s
        lo = jnp.sum(jnp.where(lo_mask, sq, 0.0), axis=-1, keepdims=True)
        hi = jnp.sum(jnp.where(lo_mask, 0.0, sq), axis=-1, keepdims=True)
        ms = jnp.where(lo_mask, lo, hi) * (2.0 / LANES)
        out_ref[:, gi * LANES:(gi + 1) * LANES] = (s * lax.rsqrt(ms + EPS) * wn).astype(out_ref.dtype)


def _nmm_qkv_kernel(x_ref, g_ref, sh_ref, sc_ref, w_ref, qn_ref, kn_ref, q_ref, k_ref, v_ref, h_scr, *, nt):
    j = pl.program_id(1)

    @pl.when(j == 0)
    def _():
        h_scr[...] = _norm_mod(x_ref[...], g_ref[...], sh_ref[...], sc_ref[...]).astype(BF16)

    acc = jnp.dot(h_scr[...], w_ref[...], preferred_element_type=F32)

    @pl.when(j < nt)
    def _():
        _qk_norm_store(acc, qn_ref[...], q_ref)

    @pl.when((j >= nt) & (j < 2 * nt))
    def _():
        _qk_norm_store(acc, kn_ref[...], k_ref)

    @pl.when(j >= 2 * nt)
    def _():
        v_ref[...] = acc


def norm_mod_matmul(x, g, shift, scale, w, rows_per_seq, tm, tn=COL_TILE):
    m, d = x.shape
    n = w.shape[1]
    return pl.pallas_call(
        _nmm_plain_kernel,
        grid=(m // tm, n // tn),
        in_specs=[pl.BlockSpec((tm, d), lambda i, j: (i, 0)),
                  pl.BlockSpec((1, d), lambda i, j: (0, 0)),
                  _mod_spec(shift, tm, rows_per_seq),
                  _mod_spec(scale, tm, rows_per_seq),
                  pl.BlockSpec((d, tn), lambda i, j: (0, j))],
        out_specs=pl.BlockSpec((tm, tn), lambda i, j: (i, j)),
        out_shape=jax.ShapeDtypeStruct((m, n), F32),
        scratch_shapes=[pltpu.VMEM((tm, d), BF16)],
        compiler_params=_cparams(("arbitrary", "arbitrary")),
        name="norm_proj",
    )(x, g, shift, scale, w)


def norm_mod_qkv(x, g, shift, scale, w, qn, kn, rows_per_seq, tm, tn=COL_TILE):
    m, d = x.shape
    part = w.shape[1] // 3
    nt = part // tn
    clamp = lambda j, lo: jnp.clip(j - lo, 0, nt - 1)
    return pl.pallas_call(
        functools.partial(_nmm_qkv_kernel, nt=nt),
        grid=(m // tm, 3 * nt),
        in_specs=[pl.BlockSpec((tm, d), lambda i, j: (i, 0)),
                  pl.BlockSpec((1, d), lambda i, j: (0, 0)),
                  _mod_spec(shift, tm, rows_per_seq),
                  _mod_spec(scale, tm, rows_per_seq),
                  pl.BlockSpec((d, tn), lambda i, j: (0, j)),
                  pl.BlockSpec((1, LANES), lambda i, j: (0, 0)),
                  pl.BlockSpec((1, LANES), lambda i, j: (0, 0))],
        out_specs=[pl.BlockSpec((tm, tn), lambda i, j: (i, clamp(j, 0))),
                   pl.BlockSpec((tm, tn), lambda i, j: (i, clamp(j, nt))),
                   pl.BlockSpec((tm, tn), lambda i, j: (i, clamp(j, 2 * nt)))],
        out_shape=[jax.ShapeDtypeStruct((m, part), BF16),
                   jax.ShapeDtypeStruct((m, part), F32),
                   jax.ShapeDtypeStruct((m, part), F32)],
        scratch_shapes=[pltpu.VMEM((tm, d), BF16)],
        compiler_params=_cparams(("arbitrary", "arbitrary")),
        name="norm_qkv",
    )(x, g, shift, scale, w, qn, kn)


def _mmres_kernel(o_ref, w_ref, x_ref, ga_ref, out_ref):
    out_ref[...] = x_ref[...] + ga_ref[...] * jnp.dot(o_ref[...], w_ref[...], preferred_element_type=F32)


def matmul_residual(o, w, x, gate, rows_per_seq, tm, tn=COL_TILE):
    m, kdim = o.shape
    n = w.shape[1]
    return pl.pallas_call(
        _mmres_kernel,
        grid=(m // tm, n // tn),
        in_specs=[pl.BlockSpec((tm, kdim), lambda i, j: (i, 0)),
                  pl.BlockSpec((kdim, tn), lambda i, j: (0, j)),
                  pl.BlockSpec((tm, tn), lambda i, j: (i, j)),
                  _mod_spec_cols(gate, tm, tn, rows_per_seq)],
        out_specs=pl.BlockSpec((tm, tn), lambda i, j: (i, j)),
        out_shape=jax.ShapeDtypeStruct((m, n), F32),
        compiler_params=_cparams(("arbitrary", "arbitrary")),
        name="out_proj_residual",
    )(o, w, x, gate)


def _nrouter_kernel(x_ref, g_ref, sh_ref, sc_ref, wh_ref, wl_ref, h_ref, lg_ref):
    h = _norm_mod(x_ref[...], g_ref[...], sh_ref[...], sc_ref[...])
    h1 = h.astype(BF16)
    h_ref[...] = h1
    h2 = (h - h1.astype(F32)).astype(BF16)
    wh = wh_ref[...]
    lg_ref[...] = (jnp.dot(h1, wh, preferred_element_type=F32)
                   + jnp.dot(h1, wl_ref[...], preferred_element_type=F32)
                   + jnp.dot(h2, wh, preferred_element_type=F32))


def norm_mod_router(x, g, shift, scale, wr_hi, wr_lo, rows_per_seq, tm):
    m, d = x.shape
    nr = wr_hi.shape[1]
    ij = lambda f: (lambda i: f(i, 0))
    sh_spec = _mod_spec(shift, tm, rows_per_seq)
    sc_spec = _mod_spec(scale, tm, rows_per_seq)
    return pl.pallas_call(
        _nrouter_kernel,
        grid=(m // tm,),
        in_specs=[pl.BlockSpec((tm, d), lambda i: (i, 0)),
                  pl.BlockSpec((1, d), lambda i: (0, 0)),
                  pl.BlockSpec(sh_spec.block_shape, ij(sh_spec.index_map)),
                  pl.BlockSpec(sc_spec.block_shape, ij(sc_spec.index_map)),
                  pl.BlockSpec((d, nr), lambda i: (0, 0)),
                  pl.BlockSpec((d, nr), lambda i: (0, 0))],
        out_specs=[pl.BlockSpec((tm, d), lambda i: (i, 0)),
                   pl.BlockSpec((tm, nr), lambda i: (i, 0))],
        out_shape=[jax.ShapeDtypeStruct((m, d), BF16),
                   jax.ShapeDtypeStruct((m, nr), F32)],
        compiler_params=_cparams(("arbitrary",)),
        name="norm_router",
    )(x, g, shift, scale, wr_hi, wr_lo)


def _chunk_cumsum(x, tri):
    c = x.shape[0]
    if tri is None:
        row = lax.broadcasted_iota(jnp.int32, x.shape, 0)
        out = jnp.zeros_like(x)
        for s in range(c):
            out = out + jnp.where(row >= s, x[s:s + 1], 0.0)
        return out
    x1, x2, x3 = _split3(x)
    return (jnp.dot(tri, x1, preferred_element_type=F32)
            + jnp.dot(tri, x2, preferred_element_type=F32)
            + jnp.dot(tri, x3, preferred_element_type=F32))


def _hgrn_kernel(q_ref, f_ref, i_ref, g_ref, lb_ref, on_ref, s0_ref, o_ref, sn_ref, st_scr, *, seq, chunk, sub, hb):
    lb = lb_ref[...]
    on = on_ref[...]
    for hh in range(hb):
        st_scr[hh] = s0_ref[hh].T
    nsub = chunk // sub
    row_sub = lax.broadcasted_iota(jnp.int32, (sub, LANES), 0)
    if chunk > BF16_ROWS:
        tri = (lax.broadcasted_iota(jnp.int32, (chunk, chunk), 0)
               >= lax.broadcasted_iota(jnp.int32, (chunk, chunk), 1)).astype(BF16)
    else:
        tri = None

    def chunk_body(ci, carry):
        r0 = pl.multiple_of(ci * chunk, chunk)
        q = q_ref[pl.ds(r0, chunk), :]
        f = f_ref[pl.ds(r0, chunk), :]
        v = i_ref[pl.ds(r0, chunk), :]
        g = g_ref[pl.ds(r0, chunk), :]
        qs_all = _silu(q)
        log_f = jnp.log(lb + (1.0 - lb) * jax.nn.sigmoid(f))
        k_all = (1.0 - lb) * jax.nn.sigmoid(-f)
        b_all = _chunk_cumsum(log_f, tri)
        sg_all = _silu(g)
        outs = []
        for hh in range(hb):
            cs = slice(hh * LANES, (hh + 1) * LANES)
            qs, k, b, vh = qs_all[:, cs], k_all[:, cs], b_all[:, cs], v[:, cs]
            st = st_scr[hh]
            vb = vh.astype(BF16)
            o_inter = lax.dot_general((qs * jnp.exp(b)).astype(BF16), st.astype(BF16), NT_DIMS,
                                      preferred_element_type=F32)
            parts = []
            for blk in range(nsub):
                lo = blk * sub
                q_b, b_b, k_b, v_b = qs[lo:lo + sub], b[lo:lo + sub], k[lo:lo + sub], vh[lo:lo + sub]
                o_b = o_inter[lo:lo + sub]
                if blk > 0:
                    b_ref_row = b[lo - 1:lo]
                    qd = (q_b * jnp.exp(b_b - b_ref_row)).astype(BF16)
                    kd = (k[:lo] * jnp.exp(b_ref_row - b[:lo])).astype(BF16)
                    sc = lax.dot_general(qd, kd, NT_DIMS, preferred_element_type=F32)
                    o_b = o_b + jnp.dot(sc.astype(BF16), vb[:lo], preferred_element_type=F32)
                for s in range(sub):
                    e = jnp.exp(b_b - b_b[s:s + 1])
                    a = jnp.where(row_sub >= s, q_b * e * k_b[s:s + 1], 0.0)
                    o_b = o_b + jnp.sum(a, axis=-1, keepdims=True) * v_b[s:s + 1]
                parts.append(o_b)
            o = parts[0] if nsub == 1 else jnp.concatenate(parts, axis=0)
            b_last = b[chunk - 1:chunk]
            kdc = (k * jnp.exp(b_last - b)).astype(BF16)
            upd = lax.dot_general(vb, kdc, TN_DIMS, preferred_element_type=F32)
            st_scr[hh] = st * jnp.exp(b_last) + upd
            outs.append(o * lax.rsqrt(jnp.mean(o * o, axis=-1, keepdims=True) + EPS) * on * sg_all[:, cs])
        y = outs[0] if hb == 1 else jnp.concatenate(outs, axis=1)
        o_ref[pl.ds(r0, chunk), :] = y.astype(o_ref.dtype)
        return carry

    lax.fori_loop(0, seq // chunk, chunk_body, 0)
    for hh in range(hb):
        sn_ref[hh] = st_scr[hh].T


def hgrn_recurrence(qfig, lb, out_norm, s0, batch, seq, out_dtype):
    heads = s0.shape[1]
    dk = s0.shape[2]
    hb = HG_HEADS_PER_STEP
    ng = heads // hb
    chunk = min(HG_CHUNK, seq)
    sub = min(HG_SUB, chunk)
    col = lambda part: (lambda b, h: (b, part * ng + h))
    return pl.pallas_call(
        functools.partial(_hgrn_kernel, seq=seq, chunk=chunk, sub=sub, hb=hb),
        grid=(batch, ng),
        in_specs=[pl.BlockSpec((seq, hb * dk), col(0)),
                  pl.BlockSpec((seq, hb * dk), col(1)),
                  pl.BlockSpec((seq, hb * dk), col(2)),
                  pl.BlockSpec((seq, hb * dk), col(3)),
                  pl.BlockSpec((1, hb * dk), lambda b, h: (0, h)),
                  pl.BlockSpec((1, dk), lambda b, h: (0, 0)),
                  pl.BlockSpec((None, hb, dk, dk), lambda b, h: (b, h, 0, 0))],
        out_specs=[pl.BlockSpec((seq, hb * dk), lambda b, h: (b, h)),
                   pl.BlockSpec((None, hb, dk, dk), lambda b, h: (b, h, 0, 0))],
        out_shape=[jax.ShapeDtypeStruct((batch * seq, heads * dk), out_dtype),
                   jax.ShapeDtypeStruct(s0.shape, F32)],
        scratch_shapes=[pltpu.VMEM((hb, dk, dk), F32)],
        compiler_params=_cparams(("arbitrary", "arbitrary")),
        name="hgrn_recurrence",
    )(qfig, qfig, qfig, qfig, lb, out_norm, s0)


def _bucket(n):
    max_exact = NUM_BUCKETS // 2
    large = max_exact + (jnp.log(jnp.maximum(n, max_exact).astype(F32) / max_exact)
                         / math.log(MAX_DISTANCE / max_exact) * (NUM_BUCKETS - max_exact)).astype(jnp.int32)
    return jnp.where(n < max_exact, n, jnp.minimum(large, NUM_BUCKETS - 1))


def _bias_tile(qpos, kpos, table):
    n = jnp.maximum(qpos[:, None] - kpos[None, :], 0)
    bias = jnp.moveaxis(table[_bucket(n)], -1, 0).astype(F32)
    return jnp.where((qpos[:, None] >= kpos[None, :])[None], bias, NEG_BIG)


def _attn_prompt_kernel(lam_ref, bc_ref, q_ref, k_ref, v_ref, bn_ref, sub_ref, o_ref, kb, vb,
                        *, tile, hb, out_scale):
    hg = pl.program_id(1)
    qi = pl.program_id(2)

    cols = [slice(hh * LANES, (hh + 1) * LANES) for hh in range(hb)]

    @pl.when(qi == 0)
    def _():
        kb[...] = k_ref[...].astype(BF16)
        for hh in range(hb):
            vb[hh] = v_ref[:, cols[hh]].T.astype(BF16)

    q = q_ref[...]
    lo_mask = lax.broadcasted_iota(jnp.int32, (tile, LANES), 1) < (LANES // 2)
    zero = jnp.zeros((tile, LANES), BF16)
    qqs = [jnp.concatenate([jnp.where(lo_mask, q[:, cs], zero), jnp.where(lo_mask, zero, q[:, cs])], axis=0)
           for cs in cols]

    nch = 2 * tile // LANES

    def step(qq, kt, vt, bias, carry):
        out = []
        for c in range(nch):
            m, l, acc = carry[c]
            b = bias(c) if callable(bias) else bias
            s = lax.dot_general(kt, qq[c * LANES:(c + 1) * LANES], NT_DIMS, preferred_element_type=F32) + b
            m_new = jnp.maximum(m, jnp.max(s, axis=0, keepdims=True))
            alpha = jnp.exp(m - m_new)
            p = jnp.exp(s - m_new)
            l = l * alpha + jnp.sum(p, axis=0, keepdims=True)
            acc = acc * alpha + jnp.dot(vt, p.astype(BF16), preferred_element_type=F32)
            out.append((m_new, l, acc))
        return tuple(out)

    def far_body(kv, carry):
        r0 = pl.multiple_of(kv * tile, tile)
        return tuple(step(qqs[hh], kb[pl.ds(r0, tile), cols[hh]], vb[hh, :, pl.ds(r0, tile)],
                          bc_ref[hg * hb + hh], carry[hh]) for hh in range(hb))

    init = tuple(tuple((jnp.full((1, LANES), NEG_BIG, F32), jnp.zeros((1, LANES), F32),
                        jnp.zeros((LANES, LANES), F32)) for _ in range(nch)) for _ in range(hb))
    n_far = jnp.maximum(qi - 1, 0)
    carry = lax.fori_loop(0, n_far, far_body, init)
    n0 = pl.multiple_of(n_far * tile, tile)
    outs = []
    for hh in range(hb):
        fin = step(qqs[hh], kb[pl.ds(n0, 2 * tile), cols[hh]], vb[hh, :, pl.ds(n0, 2 * tile)],
                   lambda c, hh=hh: bn_ref[hh, :, c * LANES:(c + 1) * LANES], carry[hh])
        o = jnp.concatenate([acc / l for _, l, acc in fin], axis=1)
        od = o[:, :tile] - lam_ref[0] * o[:, tile:]
        yt = od * lax.rsqrt(jnp.mean(od * od, axis=0, keepdims=True) + EPS)
        outs.append(yt.T * sub_ref[...] * out_scale)
    y = outs[0] if hb == 1 else jnp.concatenate(outs, axis=1)
    o_ref[...] = y.astype(o_ref.dtype)


def attn_prompt(q, k, v, lam, bias_far, bias_near, subln, batch, seq, out_scale):
    heads = q.shape[1] // LANES
    hb = ATT_HEADS_PER_STEP
    tile = min(ATT_TILE, seq)
    nq = seq // tile
    smem = pl.BlockSpec(memory_space=pltpu.SMEM)
    return pl.pallas_call(
        functools.partial(_attn_prompt_kernel, tile=tile, hb=hb, out_scale=out_scale),
        grid=(batch, heads // hb, nq),
        in_specs=[smem, smem,
                  pl.BlockSpec((tile, hb * LANES), lambda b, h, i: (b * nq + i, h)),
                  pl.BlockSpec((seq, hb * LANES), lambda b, h, i: (b, h)),
                  pl.BlockSpec((seq, hb * LANES), lambda b, h, i: (b, h)),
                  pl.BlockSpec((hb, None, 2 * tile, 2 * tile), lambda b, h, i: (h, jnp.minimum(i, 1), 0, 0)),
                  pl.BlockSpec((1, LANES), lambda b, h, i: (0, 0))],
        out_specs=pl.BlockSpec((tile, hb * LANES), lambda b, h, i: (b * nq + i, h)),
        out_shape=jax.ShapeDtypeStruct(q.shape, BF16),
        scratch_shapes=[pltpu.VMEM((seq, hb * LANES), BF16), pltpu.VMEM((hb, LANES, seq), BF16)],
        compiler_params=_cparams(("arbitrary", "arbitrary", "arbitrary")),
        name="attn_prompt",
    )(lam, bias_far, q, k, v, bias_near, subln)


def _attn_decode_kernel(pt_ref, lam_ref, q_ref, *refs, npg, heads, out_scale):
    k_refs = refs[:npg]
    v_refs = refs[npg:2 * npg]
    kn_ref, vn_ref, bf_ref, bt_ref, bn_ref, sub_ref, o_ref, m_scr, l_scr, acc_scr = refs[2 * npg:]
    p = pl.program_id(1)
    ngrp = heads // SUBLANES
    grows = q_ref.shape[0] // ngrp
    dh = q_ref.shape[1]

    @pl.when(p == 0)
    def _():
        m_scr[...] = jnp.full(m_scr.shape, NEG_BIG, F32)
        l_scr[...] = jnp.zeros(l_scr.shape, F32)
        acc_scr[...] = jnp.zeros(acc_scr.shape, F32)

    q = q_ref[...]

    def pages(k_list, v_list, b_list):
        for a in range(ngrp):
            rs = slice(a * grows, (a + 1) * grows)
            hs = slice(a * SUBLANES, (a + 1) * SUBLANES)
            ss = [lax.dot_general(q[rs], k_ref[:, hs, :].reshape(-1, dh).astype(BF16), NT_DIMS,
                                  preferred_element_type=F32) + b_ref[rs, :]
                  for k_ref, b_ref in zip(k_list, b_list)]
            m_old = m_scr[rs, :]
            m_new = m_old
            for s in ss:
                m_new = jnp.maximum(m_new, jnp.max(s, axis=-1, keepdims=True))
            alpha = jnp.exp(m_old - m_new)
            l_new = l_scr[rs, :] * alpha
            acc_new = acc_scr[rs, :] * alpha
            for s, v_ref in zip(ss, v_list):
                pe = jnp.exp(s - m_new)
                l_new = l_new + jnp.sum(pe, axis=-1, keepdims=True)
                acc_new = acc_new + jnp.dot(pe.astype(BF16), v_ref[:, hs, :].reshape(-1, dh).astype(BF16),
                                            preferred_element_type=F32)
            l_scr[rs, :] = l_new
            acc_scr[rs, :] = acc_new
            m_scr[rs, :] = m_new

    pages(k_refs, v_refs, [bf_ref] * (npg - 1) + [bt_ref])

    @pl.when(p == pl.num_programs(1) - 1)
    def _():
        pages([kn_ref], [vn_ref], [bn_ref])
        o = acc_scr[...] / l_scr[...]
        rows = q_ref.shape[0] // heads
        o4 = o.reshape(heads, 2, rows // 2, LANES)
        od = o4[:, 0] - lam_ref[0] * o4[:, 1]
        y = od * lax.rsqrt(jnp.mean(od * od, axis=-1, keepdims=True) + EPS) * sub_ref[...] * out_scale
        o_ref[...] = y


def attn_decode(qz, cache_k, cache_v, layer, k_new, v_new, page_table, lam, bias_far, bias_tail, bias_new,
                subln, out_scale):
    bd, qrows, _ = qz.shape
    _, _, page, heads, dh = cache_k.shape
    n_pages = page_table.shape[1]
    npg = PAGES_PER_STEP
    nsteps = n_pages // npg
    nq = qrows // heads // 2
    bcols = page * SUBLANES
    page_spec = lambda gi: pl.BlockSpec(
        (None, None, page, heads, dh), lambda b, p, pt: (layer, pt[b, p * npg + gi], 0, 0, 0))
    const2 = lambda b, p, pt: (0, 0)
    new_spec = pl.BlockSpec((None, page, heads, dh), lambda b, p, pt: (b, 0, 0, 0))
    grid_spec = pltpu.PrefetchScalarGridSpec(
        num_scalar_prefetch=1,
        grid=(bd, nsteps),
        in_specs=[pl.BlockSpec(memory_space=pltpu.SMEM),
                  pl.BlockSpec((None, qrows, dh), lambda b, p, pt: (b, 0, 0))]
                 + [page_spec(gi) for gi in range(npg)] + [page_spec(gi) for gi in range(npg)]
                 + [new_spec, new_spec,
                    pl.BlockSpec((qrows, bcols), const2),
                    pl.BlockSpec((None, qrows, bcols), lambda b, p, pt: (p // (nsteps - 1), 0, 0)),
                    pl.BlockSpec((qrows, bcols), const2), pl.BlockSpec((1, dh), const2)],
        out_specs=pl.BlockSpec((None, heads, nq, dh), lambda b, p, pt: (b, 0, 0, 0)),
        scratch_shapes=[pltpu.VMEM((qrows, 1), F32), pltpu.VMEM((qrows, 1), F32), pltpu.VMEM((qrows, dh), F32)],
    )
    return pl.pallas_call(
        functools.partial(_attn_decode_kernel, npg=npg, heads=heads, out_scale=out_scale),
        grid_spec=grid_spec,
        out_shape=jax.ShapeDtypeStruct((bd, heads, nq, dh), F32),
        compiler_params=_cparams(("arbitrary", "arbitrary")),
        name="attn_decode",
    )(page_table, lam, qz, *([cache_k] * npg), *([cache_v] * npg), k_new, v_new,
      bias_far, bias_tail, bias_new, subln)


def route(logits, bg, be):
    t = logits.shape[0]
    lg = logits[:, :N_GROUPS]
    le = logits[:, N_GROUPS:N_GROUPS * (1 + EXPERTS_PER_GROUP)].reshape(t, N_GROUPS, EXPERTS_PER_GROUP)
    g_sel = jnp.argmax(lg + bg.astype(F32), axis=-1)
    p_group = jnp.take_along_axis(jax.nn.softmax(lg, axis=-1), g_sel[:, None], axis=1)
    le_sel = jnp.take_along_axis(le, g_sel[:, None, None], axis=1)[:, 0]
    _, top = lax.top_k(le_sel + be.astype(F32)[g_sel], TOP_K_IN_GROUP)
    p_exp = jax.nn.softmax(jnp.take_along_axis(le_sel, top, axis=1), axis=-1)
    gate = p_group * p_exp
    expert = (g_sel[:, None] * EXPERTS_PER_GROUP + top).astype(jnp.int32)
    return expert, gate


def moe_plan(expert, n_experts, n_hid_tiles):
    t, k = expert.shape
    a = t * k
    bm = MOE_BLOCK
    flat_e = expert.reshape(-1)
    order = jnp.argsort(flat_e)
    sorted_e = flat_e[order]
    counts = jnp.bincount(flat_e, length=n_experts)
    padded = (counts + bm - 1) // bm * bm
    start = jnp.cumsum(counts) - counts
    ends_p = jnp.cumsum(padded)
    pstart = ends_p - padded
    dest_sorted = pstart[sorted_e] + jnp.arange(a) - start[sorted_e]
    dest = jnp.zeros((a,), jnp.int32).at[order].set(dest_sorted.astype(jnp.int32))
    n_blocks = (a + n_experts * (bm - 1) + bm - 1) // bm
    slot_tok = jnp.zeros((n_blocks * bm,), jnp.int32).at[dest].set((jnp.arange(a) // k).astype(jnp.int32))

    bidx = jnp.arange(n_blocks, dtype=jnp.int32)
    block_e = jnp.minimum(jnp.searchsorted(ends_p, bidx * bm, side='right'), n_experts - 1).astype(jnp.int32)
    used = (ends_p[-1] // bm).astype(jnp.int32)
    first_blk = (pstart // bm).astype(jnp.int32)
    nblk_e = (padded // bm).astype(jnp.int32)
    bi = bidx - first_blk[block_e]
    r = bi % MOE_CHUNK_BLOCKS
    cb = bidx - r
    nbc = jnp.minimum(MOE_CHUNK_BLOCKS, nblk_e[block_e] - (bi - r))
    valid_b = bidx < used
    ns = n_hid_tiles * n_blocks
    cols = []
    for ti in range(n_hid_tiles):
        step = jnp.where(valid_b, n_hid_tiles * cb + ti * nbc + r, ns)
        last = ti == n_hid_tiles - 1
        flags = 1 + (2 if ti == 0 else 0) + (4 if last else 0) + jnp.where(r == 0, 8, 0)
        vals = jnp.stack([block_e, jnp.full_like(bidx, ti), bidx, r, bidx if last else cb,
                          flags.astype(jnp.int32)])
        cols.append((step, vals))
    steps = jnp.concatenate([c[0] for c in cols])
    vals = jnp.concatenate([c[1] for c in cols], axis=1)
    table = jnp.zeros((6, ns), jnp.int32).at[:, steps].set(vals, mode='drop')
    n_valid = n_hid_tiles * used
    sidx = jnp.arange(ns)
    tail = table[:, jnp.maximum(n_valid - 1, 0)]
    tail = tail.at[5].set(0)
    table = jnp.where(sidx[None, :] < n_valid, table, tail[:, None])
    pad_ob = jnp.minimum(used + (sidx - n_valid) // n_hid_tiles, n_blocks - 1)
    is_pad = sidx >= n_valid
    table = table.at[4].set(jnp.where(is_pad, pad_ob, table[4])).at[5].set(jnp.where(is_pad, 64, table[5]))
    starts = (table[5] & 8) != 0
    run_idx = jnp.cumsum(starts.astype(jnp.int32)) - 1
    n_runs = run_idx[-1] + 1
    run_e = jnp.zeros((ns,), jnp.int32).at[run_idx].set(table[0])
    run_t = jnp.zeros((ns,), jnp.int32).at[run_idx].set(table[1])
    nxt = jnp.minimum(run_idx + 1, n_runs - 1)
    flags = table[5] + jnp.where(run_idx + 1 < n_runs, 16, 0) * (table[5] & 1) + (run_idx & 1) * 32
    table = jnp.concatenate([table[:5], flags[None], run_e[nxt][None], run_t[nxt][None]], axis=0)
    return dest, slot_tok, table


def _moe_kernel(se, st, sxb, sr, sob, sfl, nse, nst, x_ref, w1_hbm, w3_hbm, w2_hbm, o_ref,
                wf1, wf3, wf2, w1b, w3b, w2b, acc, sem, *, layer, th):
    s = pl.program_id(0)
    fl = sfl[s]
    valid = (fl & 1) != 0
    first = (fl & 2) != 0
    last = (fl & 4) != 0
    load_w = (fl & 8) != 0
    has_next = (fl & 16) != 0
    slot = (fl >> 5) & 1

    @pl.when((fl & 64) != 0)
    def _():
        o_ref[...] = jnp.zeros(o_ref.shape, o_ref.dtype)

    def weight_copies(e, t, sl):
        c0 = pl.multiple_of(t * th, th)
        return (pltpu.make_async_copy(w1_hbm.at[layer, e, :, pl.ds(c0, th)], wf1.at[sl], sem.at[0, sl]),
                pltpu.make_async_copy(w3_hbm.at[layer, e, :, pl.ds(c0, th)], wf3.at[sl], sem.at[1, sl]),
                pltpu.make_async_copy(w2_hbm.at[layer, e, pl.ds(c0, th), :], wf2.at[sl], sem.at[2, sl]))

    @pl.when(valid & load_w)
    def _():
        @pl.when(s == 0)
        def _():
            for c in weight_copies(se[s], st[s], slot):
                c.start()

        for c in weight_copies(se[s], st[s], slot):
            c.wait()

        @pl.when(has_next)
        def _():
            for c in weight_copies(nse[s], nst[s], 1 - slot):
                c.start()

        w1b[...] = wf1[slot].astype(BF16)
        w3b[...] = wf3[slot].astype(BF16)
        w2b[...] = wf2[slot].astype(BF16)

    @pl.when(valid)
    def _():
        x = x_ref[...]
        a = jnp.dot(x, w1b[...], preferred_element_type=F32)
        b = jnp.dot(x, w3b[...], preferred_element_type=F32)
        hdn = (_silu(a) * b).astype(BF16)
        y = jnp.dot(hdn, w2b[...], preferred_element_type=F32)
        r = sr[s]

        @pl.when(first & last)
        def _():
            o_ref[...] = y

        @pl.when(first & jnp.logical_not(last))
        def _():
            acc[r] = y

        @pl.when(jnp.logical_not(first) & jnp.logical_not(last))
        def _():
            acc[r] = acc[r] + y

        @pl.when(jnp.logical_not(first) & last)
        def _():
            o_ref[...] = acc[r] + y


def moe_experts(xs, table, w1, w3, w2, layer):
    np_, d = xs.shape
    de = w1.shape[-1]
    th = MOE_HID_TILE
    ns = table.shape[1]
    bm = MOE_BLOCK
    hbm = pl.BlockSpec(memory_space=pl.ANY)
    grid_spec = pltpu.PrefetchScalarGridSpec(
        num_scalar_prefetch=8,
        grid=(ns,),
        in_specs=[pl.BlockSpec((bm, d), lambda s, se, st, sxb, sr, sob, sfl, nse, nst: (sxb[s], 0)),
                  hbm, hbm, hbm],
        out_specs=pl.BlockSpec((bm, d), lambda s, se, st, sxb, sr, sob, sfl, nse, nst: (sob[s], 0)),
        scratch_shapes=[pltpu.VMEM((2, d, th), F32), pltpu.VMEM((2, d, th), F32), pltpu.VMEM((2, th, d), F32),
                        pltpu.VMEM((d, th), BF16), pltpu.VMEM((d, th), BF16), pltpu.VMEM((th, d), BF16),
                        pltpu.VMEM((MOE_CHUNK_BLOCKS, bm, d), F32),
                        pltpu.SemaphoreType.DMA((3, 2))],
    )
    return pl.pallas_call(
        functools.partial(_moe_kernel, layer=layer, th=th),
        grid_spec=grid_spec,
        out_shape=jax.ShapeDtypeStruct((np_, d), F32),
        compiler_params=_cparams(("arbitrary",)),
        name="moe_experts",
    )(*[table[i] for i in range(8)], xs, w1, w3, w2)


def _rows(a, reps):
    return jnp.repeat(a, reps, axis=0)


def kernel(x_prompt, x_sample, c_prompt, c_sample, cache_k, cache_v, state_hgrn, page_table, norm_mix, norm_ffn, ada_w, ada_b, hgrn_w_in, hgrn_lower_bound, hgrn_out_norm, hgrn_w_out, diff_w_in, diff_qk_norm, diff_lambda, diff_subln, diff_w_out, rel_bias_table, router_group_w, router_group_b, router_expert_w, router_expert_b, expert_w1, expert_w3, expert_w2):
    bp, lp, d = x_prompt.shape
    bs, ls, _ = x_sample.shape
    mp, ms = bp * lp, bs * ls
    depth = ada_w.shape[0]
    page = cache_k.shape[2]
    n_pages = page_table.shape[1]
    past_len = n_pages * page
    n_experts = expert_w1.shape[1]
    assert lp % ROW_TILE == 0 and ms % BF16_ROWS == 0
    assert MAX_DISTANCE <= min(ATT_TILE, lp) and lp >= 2 * min(ATT_TILE, lp)
    assert MAX_DISTANCE <= page and ls <= page
    assert expert_w1.shape[-1] % MOE_HID_TILE == 0
    assert n_pages % PAGES_PER_STEP == 0 and n_pages // PAGES_PER_STEP > 1

    xp = x_prompt.reshape(mp, d)
    xs = x_sample.reshape(ms, d)

    n_c = bp + bs
    pad = (-n_c) % BF16_ROWS
    c_all = jnp.concatenate([c_prompt, c_sample, jnp.zeros((pad, d), F32)], axis=0)
    mod = adaln_all(c_all, ada_w, ada_b)

    lb_prob = jax.nn.softmax(hgrn_lower_bound.astype(F32), axis=0)
    lower_bounds = jnp.cumsum(lb_prob, axis=0) - lb_prob[0]

    tile = min(ATT_TILE, lp)
    ar = jnp.arange(tile)
    near = [_bias_tile(qpos, jnp.arange(2 * tile), rel_bias_table).transpose(0, 2, 1)
            for qpos in (ar, ar + tile)]
    bias_near = jnp.stack([jnp.concatenate([b, b], axis=2) for b in near], axis=1)
    bias_far = rel_bias_table[_bucket(jnp.array(MAX_DISTANCE, jnp.int32))].astype(F32)
    pos_s = past_len + jnp.arange(ls)
    qrows = DF_HEADS * 2 * ls

    def dec_bias(b_hqk):
        hq = jnp.stack([b_hqk, b_hqk], axis=1).reshape(qrows, b_hqk.shape[-1])
        row_head = jnp.arange(qrows) // (2 * ls) % SUBLANES
        own = row_head[:, None] == jnp.arange(SUBLANES)[None, :]
        return jnp.where(own[:, None, :], hq[:, :, None], NEG_BIG).reshape(qrows, -1)

    far_hqk = jnp.broadcast_to(bias_far[:, None, None], (DF_HEADS, ls, page))
    dbias_far = dec_bias(far_hqk)
    dbias_last = dec_bias(_bias_tile(pos_s, past_len - page + jnp.arange(page), rel_bias_table))
    dbias_tail = jnp.stack([dbias_far, dbias_last])
    new_kpos = jnp.where(jnp.arange(page) < ls, past_len + jnp.arange(page), past_len + 2 * page)
    dbias_new = dec_bias(_bias_tile(pos_s, new_kpos, rel_bias_table))

    outs = {k: [] for k in ('k_p', 'v_p', 'k_s', 'v_s', 'st_p', 'st_s')}
    for i in range(depth):
        j = i // N_MIXERS
        mods = jnp.split(mod[i], 6, axis=-1)
        mp_ = [m[:bp].reshape(bp, 1, d) for m in mods]
        ms_ = [_rows(m[bp:bp + bs], ls) for m in mods]
        g_mix = norm_mix[i].reshape(1, d)
        g_ffn = norm_ffn[i].reshape(1, d)

        if i % N_MIXERS == 0:
            w_in = hgrn_w_in[j].astype(BF16)
            w_out = hgrn_w_out[j].astype(BF16)
            lb = lower_bounds[j].reshape(1, -1)
            onorm = hgrn_out_norm[j].astype(F32).reshape(1, -1)
            dk = onorm.shape[1]
            qfig_p = norm_mod_matmul(xp, g_mix, mp_[0], mp_[1], w_in, lp, ROW_TILE)
            qfig_s = norm_mod_matmul(xs, g_mix, ms_[0], ms_[1], w_in, ls, ms)
            s0_p = jnp.zeros((bp, HG_HEADS, dk, dk), F32)
            op, sp = hgrn_recurrence(qfig_p, lb, onorm, s0_p, bp, lp, BF16)
            os_, ss = hgrn_recurrence(qfig_s, lb, onorm, state_hgrn[j].astype(F32), bs, ls, F32)
            os_ = os_.astype(BF16)
            outs['st_p'].append(sp.astype(state_hgrn.dtype))
            outs['st_s'].append(ss.astype(state_hgrn.dtype))
        else:
            lam_init = 0.8 - 0.6 * math.exp(-0.3 * i)
            lpf = diff_lambda[j].astype(F32)
            lam = (jnp.exp(jnp.sum(lpf[0] * lpf[1])) - jnp.exp(jnp.sum(lpf[2] * lpf[3])) + lam_init).reshape(1)
            w_in = diff_w_in[j].astype(BF16)
            w_out = diff_w_out[j].astype(BF16)
            dh = diff_qk_norm.shape[-1]
            qn = jnp.tile(diff_qk_norm[j, 0], 2).reshape(1, 2 * dh) * (dh ** -0.5)
            kn = jnp.tile(diff_qk_norm[j, 1], 2).reshape(1, 2 * dh)
            subln = diff_subln[j].reshape(1, -1)
            q_p, k_p, v_p = norm_mod_qkv(xp, g_mix, mp_[0], mp_[1], w_in, qn, kn, lp, ROW_TILE)
            q_s, k_s, v_s = norm_mod_qkv(xs, g_mix, ms_[0], ms_[1], w_in, qn, kn, ls, ms)
            op = attn_prompt(q_p, k_p, v_p, lam, bias_far, bias_near, subln, bp, lp, 1.0 - lam_init)
            q4 = q_s.reshape(bs, ls, DF_HEADS, 2 * dh).transpose(0, 2, 1, 3)
            lo = jnp.arange(2 * dh) < dh
            qz = jnp.stack([jnp.where(lo, q4, 0), jnp.where(lo, 0, q4)], axis=2).reshape(bs, DF_HEADS * 2 * ls, 2 * dh)
            k_s5 = k_s.reshape(bs, ls, DF_HEADS, 2 * dh)
            v_s5 = v_s.reshape(bs, ls, DF_HEADS, 2 * dh)
            padw = ((0, 0), (0, page - ls), (0, 0), (0, 0))
            o_s = attn_decode(qz, cache_k, cache_v, j, jnp.pad(k_s5, padw), jnp.pad(v_s5, padw), page_table, lam,
                              dbias_far, dbias_tail, dbias_new, subln, 1.0 - lam_init)
            os_ = o_s.transpose(0, 2, 1, 3).reshape(ms, DF_HEADS * 2 * dh).astype(BF16)
            outs['k_p'].append(k_p.reshape(bp, lp, DF_HEADS, 2 * dh))
            outs['v_p'].append(v_p.reshape(bp, lp, DF_HEADS, 2 * dh))
            outs['k_s'].append(k_s5)
            outs['v_s'].append(v_s5)

        xp = matmul_residual(op, w_out, xp, mp_[2], lp, ROW_TILE)
        xs = matmul_residual(os_, w_out, xs, ms_[2], ls, ms)

        nr = LANES
        wr = jnp.concatenate([router_group_w[i], router_expert_w[i],
                              jnp.zeros((d, nr - N_GROUPS * (1 + EXPERTS_PER_GROUP)), F32)], axis=1)
        wr_hi = wr.astype(BF16)
        wr_lo = (wr - wr_hi.astype(F32)).astype(BF16)
        h_p, lg_p = norm_mod_router(xp, g_ffn, mp_[3], mp_[4], wr_hi, wr_lo, lp, ROW_TILE)
        h_s, lg_s = norm_mod_router(xs, g_ffn, ms_[3], ms_[4], wr_hi, wr_lo, ls, ms)
        h_all = jnp.concatenate([h_p, h_s], axis=0)
        expert, gate = route(jnp.concatenate([lg_p, lg_s], axis=0), router_group_b[i], router_expert_b[i])
        n_hid_tiles = expert_w1.shape[-1] // MOE_HID_TILE
        dest, slot_tok, table = moe_plan(expert, n_experts, n_hid_tiles)
        yb = moe_experts(h_all[slot_tok], table, expert_w1, expert_w3, expert_w2, i)
        y_assign = yb[dest].reshape(mp + ms, TOP_K_IN_GROUP, d)
        y = jnp.sum(gate[:, :, None] * y_assign, axis=1)
        xp = xp + (mp_[5] * y[:mp].reshape(bp, lp, d)).reshape(mp, d)
        xs = xs + ms_[5] * y[mp:]

    return (xp.reshape(bp, lp, d), xs.reshape(bs, ls, d),
            jnp.stack(outs['k_p']), jnp.stack(outs['v_p']), jnp.stack(outs['k_s']), jnp.stack(outs['v_s']),
            jnp.stack(outs['st_p']), jnp.stack(outs['st_s']))
```

```python
import functools
import math

import jax
import jax.numpy as jnp
from jax import lax
from jax.experimental import pallas as pl
from jax.experimental.pallas import tpu as pltpu

F32 = jnp.float32
BF16 = jnp.bfloat16

N_MIXERS = 2
HG_HEADS = 16
DF_HEADS = 16
NUM_BUCKETS = 32
MAX_DISTANCE = 128
N_GROUPS = 4
EXPERTS_PER_GROUP = 8
TOP_K_IN_GROUP = 2
EPS = 1e-6
NEG_BIG = -1e30

LANES = 128
SUBLANES = 8
BF16_ROWS = 16
VMEM_LIMIT = 56 * 1024 * 1024

ROW_TILE = 1024
COL_TILE = 512
COMBINE_TILE = 512
HG_CHUNK = 64
HG_SUB = 16
HG_HEADS_PER_STEP = 4
ATT_HEADS_PER_STEP = 2
ATT_TILE = 256
PAGES_PER_STEP = 4
MOE_BLOCK = 128
MOE_HID_TILE = 512
MOE_CHUNK_BLOCKS = 8

NT_DIMS = (((1,), (1,)), ((), ()))
TN_DIMS = (((0,), (0,)), ((), ()))


def _cparams(sem):
    return pltpu.CompilerParams(dimension_semantics=sem, vmem_limit_bytes=VMEM_LIMIT)


def _silu(x):
    return x * jax.nn.sigmoid(x)


def _norm_mod(x, g, shift, scale):
    y = x * lax.rsqrt(jnp.mean(x * x, axis=-1, keepdims=True) + EPS)
    return (y * g) * (1.0 + scale) + shift


def _split3(x):
    x1 = x.astype(BF16)
    r1 = x - x1.astype(F32)
    x2 = r1.astype(BF16)
    x3 = (r1 - x2.astype(F32)).astype(BF16)
    return x1, x2, x3


def _adaln_kernel(c_ref, w_ref, b_ref, o_ref):
    a = _silu(c_ref[...]).astype(BF16)
    o_ref[...] = jnp.dot(a, w_ref[...].astype(BF16), preferred_element_type=F32) + b_ref[...]


def adaln_all(c_all, ada_w, ada_b, tn=1024):
    depth, d, n = ada_w.shape
    r = c_all.shape[0]
    return pl.pallas_call(
        _adaln_kernel,
        grid=(depth, n // tn),
        in_specs=[pl.BlockSpec((r, d), lambda l, j: (0, 0)),
                  pl.BlockSpec((None, d, tn), lambda l, j: (l, 0, j)),
                  pl.BlockSpec((None, 1, tn), lambda l, j: (l, 0, j))],
        out_specs=pl.BlockSpec((None, r, tn), lambda l, j: (l, 0, j)),
        out_shape=jax.ShapeDtypeStruct((depth, r, n), F32),
        compiler_params=_cparams(("arbitrary", "arbitrary")),
        name="adaln",
    )(c_all, ada_w, ada_b.reshape(depth, 1, n))


def _mod_spec(a, tm, rows_per_seq):
    if a.ndim == 3:
        blocks_per_seq = rows_per_seq // tm
        return pl.BlockSpec((None, 1, a.shape[-1]), lambda i, j: (i // blocks_per_seq, 0, 0))
    return pl.BlockSpec((tm, a.shape[-1]), lambda i, j: (i, 0))


def _mod_spec_cols(a, tm, tn, rows_per_seq):
    if a.ndim == 3:
        blocks_per_seq = rows_per_seq // tm
        return pl.BlockSpec((None, 1, tn), lambda i, j: (i // blocks_per_seq, 0, j))
    return pl.BlockSpec((tm, tn), lambda i, j: (i, j))


def _nmm_plain_kernel(x_ref, g_ref, sh_ref, sc_ref, w_ref, o_ref, h_scr):
    @pl.when(pl.program_id(1) == 0)
    def _():
        h_scr[...] = _norm_mod(x_ref[...], g_ref[...], sh_ref[...], sc_ref[...]).astype(BF16)

    o_ref[...] = jnp.dot(h_scr[...], w_ref[...], preferred_element_type=F32)


def _qk_norm_store(acc, wn, out_ref):
    tn = acc.shape[1]
    lo_mask = lax.broadcasted_iota(jnp.int32, (1, LANES), 1) < (LANES // 2)
    for gi in range(tn // LANES):
        s = acc[:, gi * LANES:(gi + 1) * LANES]
        sq = s * s
        lo = jnp.sum(jnp.where(lo_mask, sq, 0.0), axis=-1, keepdims=True)
        hi = jnp.sum(jnp.where(lo_mask, 0.0, sq), axis=-1, keepdims=True)
        ms = jnp.where(lo_mask, lo, hi) * (2.0 / LANES)
        out_ref[:, gi * LANES:(gi + 1) * LANES] = (s * lax.rsqrt(ms + EPS) * wn).astype(out_ref.dtype)


def _nmm_qkv_kernel(x_ref, g_ref, sh_ref, sc_ref, w_ref, qn_ref, kn_ref, q_ref, k_ref, v_ref, h_scr, *, nt):
    j = pl.program_id(1)

    @pl.when(j == 0)
    def _():
        h_scr[...] = _norm_mod(x_ref[...], g_ref[...], sh_ref[...], sc_ref[...]).astype(BF16)

    acc = jnp.dot(h_scr[...], w_ref[...], preferred_element_type=F32)

    @pl.when(j < nt)
    def _():
        _qk_norm_store(acc, qn_ref[...], q_ref)

    @pl.when((j >= nt) & (j < 2 * nt))
    def _():
        _qk_norm_store(acc, kn_ref[...], k_ref)

    @pl.when(j >= 2 * nt)
    def _():
        v_ref[...] = acc


def norm_mod_matmul(x, g, shift, scale, w, rows_per_seq, tm, tn=COL_TILE):
    m, d = x.shape
    n = w.shape[1]
    return pl.pallas_call(
        _nmm_plain_kernel,
        grid=(m // tm, n // tn),
        in_specs=[pl.BlockSpec((tm, d), lambda i, j: (i, 0)),
                  pl.BlockSpec((1, d), lambda i, j: (0, 0)),
                  _mod_spec(shift, tm, rows_per_seq),
                  _mod_spec(scale, tm, rows_per_seq),
                  pl.BlockSpec((d, tn), lambda i, j: (0, j))],
        out_specs=pl.BlockSpec((tm, tn), lambda i, j: (i, j)),
        out_shape=jax.ShapeDtypeStruct((m, n), F32),
        scratch_shapes=[pltpu.VMEM((tm, d), BF16)],
        compiler_params=_cparams(("arbitrary", "arbitrary")),
        name="norm_proj",
    )(x, g, shift, scale, w)


def norm_mod_qkv(x, g, shift, scale, w, qn, kn, rows_per_seq, tm, tn=COL_TILE):
    m, d = x.shape
    part = w.shape[1] // 3
    nt = part // tn
    clamp = lambda j, lo: jnp.clip(j - lo, 0, nt - 1)
    return pl.pallas_call(
        functools.partial(_nmm_qkv_kernel, nt=nt),
        grid=(m // tm, 3 * nt),
        in_specs=[pl.BlockSpec((tm, d), lambda i, j: (i, 0)),
                  pl.BlockSpec((1, d), lambda i, j: (0, 0)),
                  _mod_spec(shift, tm, rows_per_seq),
                  _mod_spec(scale, tm, rows_per_seq),
                  pl.BlockSpec((d, tn), lambda i, j: (0, j)),
                  pl.BlockSpec((1, LANES), lambda i, j: (0, 0)),
                  pl.BlockSpec((1, LANES), lambda i, j: (0, 0))],
        out_specs=[pl.BlockSpec((tm, tn), lambda i, j: (i, clamp(j, 0))),
                   pl.BlockSpec((tm, tn), lambda i, j: (i, clamp(j, nt))),
                   pl.BlockSpec((tm, tn), lambda i, j: (i, clamp(j, 2 * nt)))],
        out_shape=[jax.ShapeDtypeStruct((m, part), BF16),
                   jax.ShapeDtypeStruct((m, part), F32),
                   jax.ShapeDtypeStruct((m, part), F32)],
        scratch_shapes=[pltpu.VMEM((tm, d), BF16)],
        compiler_params=_cparams(("arbitrary", "arbitrary")),
        name="norm_qkv",
    )(x, g, shift, scale, w, qn, kn)


def _mmres_kernel(o_ref, w_ref, x_ref, ga_ref, out_ref):
    out_ref[...] = x_ref[...] + ga_ref[...] * jnp.dot(o_ref[...], w_ref[...], preferred_element_type=F32)


def matmul_residual(o, w, x, gate, rows_per_seq, tm, tn=COL_TILE):
    m, kdim = o.shape
    n = w.shape[1]
    return pl.pallas_call(
        _mmres_kernel,
        grid=(m // tm, n // tn),
        in_specs=[pl.BlockSpec((tm, kdim), lambda i, j: (i, 0)),
                  pl.BlockSpec((kdim, tn), lambda i, j: (0, j)),
                  pl.BlockSpec((tm, tn), lambda i, j: (i, j)),
                  _mod_spec_cols(gate, tm, tn, rows_per_seq)],
        out_specs=pl.BlockSpec((tm, tn), lambda i, j: (i, j)),
        out_shape=jax.ShapeDtypeStruct((m, n), F32),
        compiler_params=_cparams(("arbitrary", "arbitrary")),
        name="out_proj_residual",
    )(o, w, x, gate)


def _combine_kernel(x_ref, ga_ref, y0_ref, y1_ref, g0_ref, g1_ref, o_ref):
    o_ref[...] = x_ref[...] + ga_ref[...] * (g0_ref[...] * y0_ref[...] + g1_ref[...] * y1_ref[...])


def moe_combine(x, gate_mod, y0, y1, g0, g1, row0, rows_per_seq, tm):
    m, d = x.shape
    b0 = row0 // tm
    tok = lambda i, j: (b0 + i, 0)
    return pl.pallas_call(
        _combine_kernel,
        grid=(m // tm, 1),
        in_specs=[pl.BlockSpec((tm, d), lambda i, j: (i, 0)),
                  _mod_spec_cols(gate_mod, tm, d, rows_per_seq),
                  pl.BlockSpec((tm, d), tok), pl.BlockSpec((tm, d), tok),
                  pl.BlockSpec((tm, 1), tok), pl.BlockSpec((tm, 1), tok)],
        out_specs=pl.BlockSpec((tm, d), lambda i, j: (i, 0)),
        out_shape=jax.ShapeDtypeStruct((m, d), F32),
        compiler_params=_cparams(("arbitrary", "arbitrary")),
        name="moe_combine",
    )(x, gate_mod, y0, y1, g0, g1)


def _to_heads_kernel(x_ref, o_ref):
    for h in range(o_ref.shape[1]):
        o_ref[:, h, :] = x_ref[:, h * LANES:(h + 1) * LANES]


def to_heads(x, heads, tm):
    m, width = x.shape
    return pl.pallas_call(
        _to_heads_kernel,
        grid=(m // tm,),
        in_specs=[pl.BlockSpec((tm, width), lambda i: (i, 0))],
        out_specs=pl.BlockSpec((tm, heads, LANES), lambda i: (i, 0, 0)),
        out_shape=jax.ShapeDtypeStruct((m, heads, LANES), x.dtype),
        compiler_params=_cparams(("arbitrary",)),
        name="to_heads",
    )(x)


def _nrouter_kernel(x_ref, g_ref, sh_ref, sc_ref, wh_ref, wl_ref, h_ref, lg_ref, *, n_in_blocks):
    i = pl.program_id(0)

    @pl.when(i < n_in_blocks)
    def _():
        h = _norm_mod(x_ref[...], g_ref[...], sh_ref[...], sc_ref[...])
        h_ref[...] = h
        h1 = h.astype(BF16)
        h2 = (h - h1.astype(F32)).astype(BF16)
        wh = wh_ref[...]
        lg_ref[...] = (jnp.dot(h1, wh, preferred_element_type=F32)
                       + jnp.dot(h1, wl_ref[...], preferred_element_type=F32)
                       + jnp.dot(h2, wh, preferred_element_type=F32))

    @pl.when(i >= n_in_blocks)
    def _():
        h_ref[...] = jnp.zeros(h_ref.shape, h_ref.dtype)
        lg_ref[...] = jnp.zeros(lg_ref.shape, lg_ref.dtype)


def norm_mod_router(x, g, shift, scale, wr_hi, wr_lo, rows_per_seq, tm, out_rows):
    m, d = x.shape
    nr = wr_hi.shape[1]
    nb = m // tm
    clamp = lambda f: (lambda i: f(jnp.minimum(i, nb - 1), 0))
    sh_spec = _mod_spec(shift, tm, rows_per_seq)
    sc_spec = _mod_spec(scale, tm, rows_per_seq)
    return pl.pallas_call(
        functools.partial(_nrouter_kernel, n_in_blocks=nb),
        grid=(pl.cdiv(out_rows, tm),),
        in_specs=[pl.BlockSpec((tm, d), clamp(lambda i, j: (i, 0))),
                  pl.BlockSpec((1, d), lambda i: (0, 0)),
                  pl.BlockSpec(sh_spec.block_shape, clamp(sh_spec.index_map)),
                  pl.BlockSpec(sc_spec.block_shape, clamp(sc_spec.index_map)),
                  pl.BlockSpec((d, nr), lambda i: (0, 0)),
                  pl.BlockSpec((d, nr), lambda i: (0, 0))],
        out_specs=[pl.BlockSpec((tm, d), lambda i: (i, 0)),
                   pl.BlockSpec((tm, nr), lambda i: (i, 0))],
        out_shape=[jax.ShapeDtypeStruct((out_rows, d), F32),
                   jax.ShapeDtypeStruct((out_rows, nr), F32)],
        compiler_params=_cparams(("arbitrary",)),
        name="norm_router",
    )(x, g, shift, scale, wr_hi, wr_lo)


def _chunk_cumsum(x, tri):
    c = x.shape[0]
    if tri is None:
        row = lax.broadcasted_iota(jnp.int32, x.shape, 0)
        out = jnp.zeros_like(x)
        for s in range(c):
            out = out + jnp.where(row >= s, x[s:s + 1], 0.0)
        return out
    x1, x2, x3 = _split3(x)
    return (jnp.dot(tri, x1, preferred_element_type=F32)
            + jnp.dot(tri, x2, preferred_element_type=F32)
            + jnp.dot(tri, x3, preferred_element_type=F32))


def _hgrn_kernel(q_ref, f_ref, i_ref, g_ref, lb_ref, on_ref, s0_ref, o_ref, sn_ref, st_scr, *, seq, chunk, sub, hb):
    lb = lb_ref[...]
    on = on_ref[...]
    for hh in range(hb):
        st_scr[hh] = s0_ref[hh].T
    nsub = chunk // sub
    row_sub = lax.broadcasted_iota(jnp.int32, (sub, LANES), 0)
    if chunk > BF16_ROWS:
        tri = (lax.broadcasted_iota(jnp.int32, (chunk, chunk), 0)
               >= lax.broadcasted_iota(jnp.int32, (chunk, chunk), 1)).astype(BF16)
    else:
        tri = None

    def chunk_body(ci, carry):
        r0 = pl.multiple_of(ci * chunk, chunk)
        q = q_ref[pl.ds(r0, chunk), :]
        f = f_ref[pl.ds(r0, chunk), :]
        v = i_ref[pl.ds(r0, chunk), :]
        g = g_ref[pl.ds(r0, chunk), :]
        qs_all = _silu(q)
        log_f = jnp.log(lb + (1.0 - lb) * jax.nn.sigmoid(f))
        k_all = (1.0 - lb) * jax.nn.sigmoid(-f)
        b_all = _chunk_cumsum(log_f, tri)
        sg_all = _silu(g)
        outs = []
        for hh in range(hb):
            cs = slice(hh * LANES, (hh + 1) * LANES)
            qs, k, b, vh = qs_all[:, cs], k_all[:, cs], b_all[:, cs], v[:, cs]
            st = st_scr[hh]
            vb = vh.astype(BF16)
            o_inter = lax.dot_general((qs * jnp.exp(b)).astype(BF16), st.astype(BF16), NT_DIMS,
                                      preferred_element_type=F32)
            parts = []
            for blk in range(nsub):
                lo = blk * sub
                q_b, b_b, k_b, v_b = qs[lo:lo + sub], b[lo:lo + sub], k[lo:lo + sub], vh[lo:lo + sub]
                o_b = o_inter[lo:lo + sub]
                if blk > 0:
                    b_ref_row = b[lo - 1:lo]
                    qd = (q_b * jnp.exp(b_b - b_ref_row)).astype(BF16)
                    kd = (k[:lo] * jnp.exp(b_ref_row - b[:lo])).astype(BF16)
                    sc = lax.dot_general(qd, kd, NT_DIMS, preferred_element_type=F32)
                    o_b = o_b + jnp.dot(sc.astype(BF16), vb[:lo], preferred_element_type=F32)
                for s in range(sub):
                    e = jnp.exp(b_b - b_b[s:s + 1])
                    a = jnp.where(row_sub >= s, q_b * e * k_b[s:s + 1], 0.0)
                    o_b = o_b + jnp.sum(a, axis=-1, keepdims=True) * v_b[s:s + 1]
                parts.append(o_b)
            o = parts[0] if nsub == 1 else jnp.concatenate(parts, axis=0)
            b_last = b[chunk - 1:chunk]
            kdc = (k * jnp.exp(b_last - b)).astype(BF16)
            upd = lax.dot_general(vb, kdc, TN_DIMS, preferred_element_type=F32)
            st_scr[hh] = st * jnp.exp(b_last) + upd
            outs.append(o * lax.rsqrt(jnp.mean(o * o, axis=-1, keepdims=True) + EPS) * on * sg_all[:, cs])
        y = outs[0] if hb == 1 else jnp.concatenate(outs, axis=1)
        o_ref[pl.ds(r0, chunk), :] = y.astype(o_ref.dtype)
        return carry

    lax.fori_loop(0, seq // chunk, chunk_body, 0)
    for hh in range(hb):
        sn_ref[hh] = st_scr[hh].T


def hgrn_recurrence(qfig, lb, out_norm, s0, batch, seq, out_dtype):
    heads = s0.shape[1]
    dk = s0.shape[2]
    hb = HG_HEADS_PER_STEP
    ng = heads // hb
    chunk = min(HG_CHUNK, seq)
    sub = min(HG_SUB, chunk)
    col = lambda part: (lambda b, h: (b, part * ng + h))
    return pl.pallas_call(
        functools.partial(_hgrn_kernel, seq=seq, chunk=chunk, sub=sub, hb=hb),
        grid=(batch, ng),
        in_specs=[pl.BlockSpec((seq, hb * dk), col(0)),
                  pl.BlockSpec((seq, hb * dk), col(1)),
                  pl.BlockSpec((seq, hb * dk), col(2)),
                  pl.BlockSpec((seq, hb * dk), col(3)),
                  pl.BlockSpec((1, hb * dk), lambda b, h: (0, h)),
                  pl.BlockSpec((1, dk), lambda b, h: (0, 0)),
                  pl.BlockSpec((None, hb, dk, dk), lambda b, h: (b, h, 0, 0))],
        out_specs=[pl.BlockSpec((seq, hb * dk), lambda b, h: (b, h)),
                   pl.BlockSpec((None, hb, dk, dk), lambda b, h: (b, h, 0, 0))],
        out_shape=[jax.ShapeDtypeStruct((batch * seq, heads * dk), out_dtype),
                   jax.ShapeDtypeStruct(s0.shape, F32)],
        scratch_shapes=[pltpu.VMEM((hb, dk, dk), F32)],
        compiler_params=_cparams(("arbitrary", "arbitrary")),
        name="hgrn_recurrence",
    )(qfig, qfig, qfig, qfig, lb, out_norm, s0)


def _bucket(n):
    max_exact = NUM_BUCKETS // 2
    large = max_exact + (jnp.log(jnp.maximum(n, max_exact).astype(F32) / max_exact)
                         / math.log(MAX_DISTANCE / max_exact) * (NUM_BUCKETS - max_exact)).astype(jnp.int32)
    return jnp.where(n < max_exact, n, jnp.minimum(large, NUM_BUCKETS - 1))


def _bias_tile(qpos, kpos, table):
    n = jnp.maximum(qpos[:, None] - kpos[None, :], 0)
    bias = jnp.moveaxis(table[_bucket(n)], -1, 0).astype(F32)
    return jnp.where((qpos[:, None] >= kpos[None, :])[None], bias, NEG_BIG)


def _bias_by_distance(n, table):
    b = table[_bucket(jnp.maximum(n, 0))].astype(F32).T
    return jnp.where(n[None, :] >= 0, b, NEG_BIG)


def _toeplitz(gvec, nrow, ncol):
    h, p = gvec.shape
    skew = jnp.tile(gvec, (1, nrow))[:, :nrow * (p - 1)].reshape(h, nrow, p - 1)
    return skew[:, :, nrow - 1:nrow - 1 + ncol]


def _attn_prompt_kernel(lam_ref, bc_ref, q_ref, k_ref, v_ref, bn_ref, sub_ref, o_ref, kb, vb,
                        *, tile, hb, out_scale):
    hg = pl.program_id(1)
    qi = pl.program_id(2)

    cols = [slice(hh * LANES, (hh + 1) * LANES) for hh in range(hb)]

    @pl.when(qi == 0)
    def _():
        kb[...] = k_ref[...].astype(BF16)
        for hh in range(hb):
            vb[hh] = v_ref[:, cols[hh]].T.astype(BF16)

    q = q_ref[...].astype(F32)
    lo_mask = lax.broadcasted_iota(jnp.int32, (LANES, tile), 0) < (LANES // 2)
    qqs = []
    for cs in cols:
        qt = q[:, cs].T
        qqs.append(jnp.concatenate([jnp.where(lo_mask, qt, 0.0), jnp.where(lo_mask, 0.0, qt)], axis=1).astype(BF16))

    nch = 2 * tile // LANES

    def step(qq, kt, vt, bias, carry):
        out = []
        for c in range(nch):
            m, l, acc = carry[c]
            b = bias(c) if callable(bias) else bias
            s = jnp.dot(kt, qq[:, c * LANES:(c + 1) * LANES], preferred_element_type=F32) + b
            m_new = jnp.maximum(m, jnp.max(s, axis=0, keepdims=True))
            alpha = jnp.exp(m - m_new)
            p = jnp.exp(s - m_new)
            l = l * alpha + jnp.sum(p, axis=0, keepdims=True)
            acc = acc * alpha + jnp.dot(vt, p.astype(BF16), preferred_element_type=F32)
            out.append((m_new, l, acc))
        return tuple(out)

    def far_body(kv, carry):
        r0 = pl.multiple_of(kv * tile, tile)
        return tuple(step(qqs[hh], kb[pl.ds(r0, tile), cols[hh]], vb[hh, :, pl.ds(r0, tile)],
                          bc_ref[hg * hb + hh], carry[hh]) for hh in range(hb))

    init = tuple(tuple((jnp.full((1, LANES), NEG_BIG, F32), jnp.zeros((1, LANES), F32),
                        jnp.zeros((LANES, LANES), F32)) for _ in range(nch)) for _ in range(hb))
    n_far = jnp.maximum(qi - 1, 0)
    carry = lax.fori_loop(0, n_far, far_body, init)
    n0 = pl.multiple_of(n_far * tile, tile)
    outs = []
    for hh in range(hb):
        fin = step(qqs[hh], kb[pl.ds(n0, 2 * tile), cols[hh]], vb[hh, :, pl.ds(n0, 2 * tile)],
                   lambda c, hh=hh: bn_ref[hh, :, c * LANES:(c + 1) * LANES], carry[hh])
        o = jnp.concatenate([acc / l for _, l, acc in fin], axis=1)
        od = o[:, :tile] - lam_ref[0] * o[:, tile:]
        yt = od * lax.rsqrt(jnp.mean(od * od, axis=0, keepdims=True) + EPS)
        outs.append(yt.T * sub_ref[...] * out_scale)
    y = outs[0] if hb == 1 else jnp.concatenate(outs, axis=1)
    o_ref[...] = y.astype(o_ref.dtype)


def attn_prompt(q, k, v, lam, bias_far, bias_near, subln, batch, seq, out_scale):
    heads = q.shape[1] // LANES
    hb = ATT_HEADS_PER_STEP
    tile = min(ATT_TILE, seq)
    nq = seq // tile
    smem = pl.BlockSpec(memory_space=pltpu.SMEM)
    return pl.pallas_call(
        functools.partial(_attn_prompt_kernel, tile=tile, hb=hb, out_scale=out_scale),
        grid=(batch, heads // hb, nq),
        in_specs=[smem, smem,
                  pl.BlockSpec((tile, hb * LANES), lambda b, h, i: (b * nq + i, h)),
                  pl.BlockSpec((seq, hb * LANES), lambda b, h, i: (b, h)),
                  pl.BlockSpec((seq, hb * LANES), lambda b, h, i: (b, h)),
                  pl.BlockSpec((hb, None, 2 * tile, 2 * tile), lambda b, h, i: (h, jnp.minimum(i, 1), 0, 0)),
                  pl.BlockSpec((1, LANES), lambda b, h, i: (0, 0))],
        out_specs=pl.BlockSpec((tile, hb * LANES), lambda b, h, i: (b * nq + i, h)),
        out_shape=jax.ShapeDtypeStruct(q.shape, BF16),
        scratch_shapes=[pltpu.VMEM((seq, hb * LANES), BF16), pltpu.VMEM((hb, LANES, seq), BF16)],
        compiler_params=_cparams(("arbitrary", "arbitrary", "arbitrary")),
        name="attn_prompt",
    )(lam, bias_far, q, k, v, bias_near, subln)


def _attn_decode_kernel(pt_ref, lam_ref, q_ref, *refs, npg, heads, out_scale):
    k_refs = refs[:npg]
    v_refs = refs[npg:2 * npg]
    kn_ref, vn_ref, bf_ref, bt_ref, bn_ref, sub_ref, o_ref, m_scr, l_scr, acc_scr = refs[2 * npg:]
    p = pl.program_id(1)
    ngrp = heads // SUBLANES
    grows = q_ref.shape[0] // ngrp
    dh = q_ref.shape[1]

    @pl.when(p == 0)
    def _():
        m_scr[...] = jnp.full(m_scr.shape, NEG_BIG, F32)
        l_scr[...] = jnp.zeros(l_scr.shape, F32)
        acc_scr[...] = jnp.zeros(acc_scr.shape, F32)

    q = q_ref[...]

    def pages(k_list, v_list, b_list):
        for a in range(ngrp):
            rs = slice(a * grows, (a + 1) * grows)
            hs = slice(a * SUBLANES, (a + 1) * SUBLANES)
            ss = [lax.dot_general(q[rs], k_ref[:, hs, :].reshape(-1, dh).astype(BF16), NT_DIMS,
                                  preferred_element_type=F32) + b_ref[rs, :]
                  for k_ref, b_ref in zip(k_list, b_list)]
            m_old = m_scr[rs, :]
            m_new = m_old
            for s in ss:
                m_new = jnp.maximum(m_new, jnp.max(s, axis=-1, keepdims=True))
            alpha = jnp.exp(m_old - m_new)
            l_new = l_scr[rs, :] * alpha
            acc_new = acc_scr[rs, :] * alpha
            for s, v_ref in zip(ss, v_list):
                pe = jnp.exp(s - m_new)
                l_new = l_new + jnp.sum(pe, axis=-1, keepdims=True)
                acc_new = acc_new + jnp.dot(pe.astype(BF16), v_ref[:, hs, :].reshape(-1, dh).astype(BF16),
                                            preferred_element_type=F32)
            l_scr[rs, :] = l_new
            acc_scr[rs, :] = acc_new
            m_scr[rs, :] = m_new

    pages(k_refs, v_refs, [bf_ref] * (npg - 1) + [bt_ref])

    @pl.when(p == pl.num_programs(1) - 1)
    def _():
        pages([kn_ref], [vn_ref], [bn_ref])
        o = acc_scr[...] / l_scr[...]
        rows = q_ref.shape[0] // heads
        o4 = o.reshape(heads, 2, rows // 2, LANES)
        od = o4[:, 0] - lam_ref[0] * o4[:, 1]
        y = od * lax.rsqrt(jnp.mean(od * od, axis=-1, keepdims=True) + EPS) * sub_ref[...] * out_scale
        o_ref[...] = y


def attn_decode(qz, cache_k, cache_v, layer, k_new, v_new, page_table, lam, bias_far, bias_tail, bias_new,
                subln, out_scale):
    bd, qrows, _ = qz.shape
    _, _, page, heads, dh = cache_k.shape
    n_pages = page_table.shape[1]
    npg = PAGES_PER_STEP
    nsteps = n_pages // npg
    nq = qrows // heads // 2
    bcols = page * SUBLANES
    page_spec = lambda gi: pl.BlockSpec(
        (None, None, page, heads, dh), lambda b, p, pt: (layer, pt[b, p * npg + gi], 0, 0, 0))
    const2 = lambda b, p, pt: (0, 0)
    new_spec = pl.BlockSpec((None, page, heads, dh), lambda b, p, pt: (b, 0, 0, 0))
    grid_spec = pltpu.PrefetchScalarGridSpec(
        num_scalar_prefetch=1,
        grid=(bd, nsteps),
        in_specs=[pl.BlockSpec(memory_space=pltpu.SMEM),
                  pl.BlockSpec((None, qrows, dh), lambda b, p, pt: (b, 0, 0))]
                 + [page_spec(gi) for gi in range(npg)] + [page_spec(gi) for gi in range(npg)]
                 + [new_spec, new_spec,
                    pl.BlockSpec((qrows, bcols), const2),
                    pl.BlockSpec((None, qrows, bcols), lambda b, p, pt: (p // (nsteps - 1), 0, 0)),
                    pl.BlockSpec((qrows, bcols), const2), pl.BlockSpec((1, dh), const2)],
        out_specs=pl.BlockSpec((None, heads, nq, dh), lambda b, p, pt: (b, 0, 0, 0)),
        scratch_shapes=[pltpu.VMEM((qrows, 1), F32), pltpu.VMEM((qrows, 1), F32), pltpu.VMEM((qrows, dh), F32)],
    )
    return pl.pallas_call(
        functools.partial(_attn_decode_kernel, npg=npg, heads=heads, out_scale=out_scale),
        grid_spec=grid_spec,
        out_shape=jax.ShapeDtypeStruct((bd, heads, nq, dh), F32),
        compiler_params=_cparams(("arbitrary", "arbitrary")),
        name="attn_decode",
    )(page_table, lam, qz, *([cache_k] * npg), *([cache_v] * npg), k_new, v_new,
      bias_far, bias_tail, bias_new, subln)


def route(logits, bg, be):
    t = logits.shape[0]
    lg = logits[:, :N_GROUPS]
    le = logits[:, N_GROUPS:N_GROUPS * (1 + EXPERTS_PER_GROUP)].reshape(t, N_GROUPS, EXPERTS_PER_GROUP)
    g_sel = jnp.argmax(lg + bg.astype(F32), axis=-1)
    p_group = jnp.take_along_axis(jax.nn.softmax(lg, axis=-1), g_sel[:, None], axis=1)
    le_sel = jnp.take_along_axis(le, g_sel[:, None, None], axis=1)[:, 0]
    _, top = lax.top_k(le_sel + be.astype(F32)[g_sel], TOP_K_IN_GROUP)
    p_exp = jax.nn.softmax(jnp.take_along_axis(le_sel, top, axis=1), axis=-1)
    gate = p_group * p_exp
    expert = (g_sel[:, None] * EXPERTS_PER_GROUP + top).astype(jnp.int32)
    return expert, gate


def moe_plan(expert, n_experts, n_hid_tiles):
    t, k = expert.shape
    a = t * k
    bm = MOE_BLOCK
    nt = n_hid_tiles
    i32 = jnp.int32
    flat_e = expert.reshape(-1)
    eids = jnp.arange(n_experts, dtype=i32)
    onehot = (flat_e[:, None] == eids[None, :]).astype(i32)
    csum = jnp.cumsum(onehot, axis=0)
    rank = jnp.sum(csum * onehot, axis=1) - 1
    counts = csum[-1]
    padded = (counts + bm - 1) // bm * bm
    ends_p = jnp.cumsum(padded)
    pstart = ends_p - padded
    dest = (jnp.sum(onehot * pstart[None, :], axis=1) + rank).astype(i32)
    n_blocks = (a + n_experts * (bm - 1) + bm - 1) // bm
    slot_tok = jnp.zeros((n_blocks * bm,), i32).at[dest].set((jnp.arange(a) // k).astype(i32))

    bidx = jnp.arange(n_blocks, dtype=i32)
    eoh = ((bidx[:, None] * bm >= pstart[None, :]) & (bidx[:, None] * bm < ends_p[None, :])).astype(i32)
    block_e = jnp.sum(eoh * eids[None, :], axis=1)
    used = (ends_p[-1] // bm).astype(i32)
    bi = bidx - jnp.sum(eoh * (pstart // bm)[None, :], axis=1)
    r = bi % MOE_CHUNK_BLOCKS
    cb = bidx - r
    nbc = jnp.minimum(MOE_CHUNK_BLOCKS, jnp.sum(eoh * (padded // bm)[None, :], axis=1) - (bi - r))
    valid_b = bidx < used
    ns = nt * n_blocks
    n_valid = nt * used
    sidx = jnp.arange(ns, dtype=i32)
    s_eff = jnp.minimum(sidx, jnp.maximum(n_valid - 1, 0))
    reach = (nt * cb[None, :] <= s_eff[:, None]) & valid_b[None, :]
    cb_s = jnp.max(jnp.where(reach, cb[None, :], 0), axis=1)
    at_cb = (bidx[None, :] == cb_s[:, None]).astype(i32)
    nbc_s = jnp.sum(at_cb * nbc[None, :], axis=1)
    e_s = jnp.sum(at_cb * block_e[None, :], axis=1)
    off = s_eff - nt * cb_s
    t_s = off // jnp.maximum(nbc_s, 1)
    r_s = off - t_s * nbc_s
    b_s = cb_s + r_s
    is_pad = sidx >= n_valid
    last = t_s == nt - 1
    pad_ob = jnp.minimum(used + (sidx - n_valid) // nt, n_blocks - 1)
    ob_s = jnp.where(is_pad, pad_ob, jnp.where(last, b_s, cb_s))
    nxt_cb = cb_s + nbc_s
    nxt_e_chunk = jnp.sum((bidx[None, :] == nxt_cb[:, None]).astype(i32) * block_e[None, :], axis=1)
    has_next = jnp.logical_not(last) | (nxt_cb < used)
    nse = jnp.where(last, nxt_e_chunk, e_s)
    nst = jnp.where(last, 0, t_s + 1)
    chunks_before = jnp.sum(((bidx[None, :] < cb_s[:, None]) & (r[None, :] == 0)).astype(i32), axis=1)
    run_parity = (chunks_before * nt + t_s) & 1
    flags = jnp.where(is_pad, 64,
                      1 + jnp.where(t_s == 0, 2, 0) + jnp.where(last, 4, 0) + jnp.where(r_s == 0, 8, 0)
                      + jnp.where(has_next, 16, 0)) + run_parity * 32
    table = jnp.stack([e_s, t_s, b_s, r_s, ob_s, flags, nse, nst]).astype(i32)
    return dest, slot_tok, table


def _moe_kernel(se, st, sxb, sr, sob, sfl, nse, nst, x_ref, w1_hbm, w3_hbm, w2_hbm, o_ref,
                wf1, wf3, wf2, w1b, w3b, w2b, acc, sem, *, layer, th):
    s = pl.program_id(0)
    fl = sfl[s]
    valid = (fl & 1) != 0
    first = (fl & 2) != 0
    last = (fl & 4) != 0
    load_w = (fl & 8) != 0
    has_next = (fl & 16) != 0
    slot = (fl >> 5) & 1

    @pl.when((fl & 64) != 0)
    def _():
        o_ref[...] = jnp.zeros(o_ref.shape, o_ref.dtype)

    def weight_copies(e, t, sl):
        c0 = pl.multiple_of(t * th, th)
        return (pltpu.make_async_copy(w1_hbm.at[layer, e, :, pl.ds(c0, th)], wf1.at[sl], sem.at[0, sl]),
                pltpu.make_async_copy(w3_hbm.at[layer, e, :, pl.ds(c0, th)], wf3.at[sl], sem.at[1, sl]),
                pltpu.make_async_copy(w2_hbm.at[layer, e, pl.ds(c0, th), :], wf2.at[sl], sem.at[2, sl]))

    @pl.when(valid & load_w)
    def _():
        @pl.when(s == 0)
        def _():
            for c in weight_copies(se[s], st[s], slot):
                c.start()

        for c in weight_copies(se[s], st[s], slot):
            c.wait()

        @pl.when(has_next)
        def _():
            for c in weight_copies(nse[s], nst[s], 1 - slot):
                c.start()

        w1b[...] = wf1[slot].astype(BF16)
        w3b[...] = wf3[slot].astype(BF16)
        w2b[...] = wf2[slot].astype(BF16)

    @pl.when(valid)
    def _():
        x = x_ref[...].astype(BF16)
        a = jnp.dot(x, w1b[...], preferred_element_type=F32)
        b = jnp.dot(x, w3b[...], preferred_element_type=F32)
        hdn = (_silu(a) * b).astype(BF16)
        y = jnp.dot(hdn, w2b[...], preferred_element_type=F32)
        r = sr[s]

        @pl.when(first & last)
        def _():
            o_ref[...] = y

        @pl.when(first & jnp.logical_not(last))
        def _():
            acc[r] = y

        @pl.when(jnp.logical_not(first) & jnp.logical_not(last))
        def _():
            acc[r] = acc[r] + y

        @pl.when(jnp.logical_not(first) & last)
        def _():
            o_ref[...] = acc[r] + y


def moe_experts(xs, table, w1, w3, w2, layer):
    np_, d = xs.shape
    de = w1.shape[-1]
    th = MOE_HID_TILE
    ns = table.shape[1]
    bm = MOE_BLOCK
    hbm = pl.BlockSpec(memory_space=pl.ANY)
    grid_spec = pltpu.PrefetchScalarGridSpec(
        num_scalar_prefetch=8,
        grid=(ns,),
        in_specs=[pl.BlockSpec((bm, d), lambda s, se, st, sxb, sr, sob, sfl, nse, nst: (sxb[s], 0)),
                  hbm, hbm, hbm],
        out_specs=pl.BlockSpec((bm, d), lambda s, se, st, sxb, sr, sob, sfl, nse, nst: (sob[s], 0)),
        scratch_shapes=[pltpu.VMEM((2, d, th), F32), pltpu.VMEM((2, d, th), F32), pltpu.VMEM((2, th, d), F32),
                        pltpu.VMEM((d, th), BF16), pltpu.VMEM((d, th), BF16), pltpu.VMEM((th, d), BF16),
                        pltpu.VMEM((MOE_CHUNK_BLOCKS, bm, d), F32),
                        pltpu.SemaphoreType.DMA((3, 2))],
    )
    return pl.pallas_call(
        functools.partial(_moe_kernel, layer=layer, th=th),
        grid_spec=grid_spec,
        out_shape=jax.ShapeDtypeStruct((np_, d), F32),
        compiler_params=_cparams(("arbitrary",)),
        name="moe_experts",
    )(*[table[i] for i in range(8)], xs, w1, w3, w2)


def _rows(a, reps):
    return jnp.repeat(a, reps, axis=0)


def kernel(x_prompt, x_sample, c_prompt, c_sample, cache_k, cache_v, state_hgrn, page_table, norm_mix, norm_ffn, ada_w, ada_b, hgrn_w_in, hgrn_lower_bound, hgrn_out_norm, hgrn_w_out, diff_w_in, diff_qk_norm, diff_lambda, diff_subln, diff_w_out, rel_bias_table, router_group_w, router_group_b, router_expert_w, router_expert_b, expert_w1, expert_w3, expert_w2):
    bp, lp, d = x_prompt.shape
    bs, ls, _ = x_sample.shape
    mp, ms = bp * lp, bs * ls
    depth = ada_w.shape[0]
    page = cache_k.shape[2]
    n_pages = page_table.shape[1]
    past_len = n_pages * page
    n_experts = expert_w1.shape[1]
    assert lp % ROW_TILE == 0 and ms % BF16_ROWS == 0
    assert MAX_DISTANCE <= min(ATT_TILE, lp) and lp >= 2 * min(ATT_TILE, lp)
    assert MAX_DISTANCE <= page and ls <= page
    assert expert_w1.shape[-1] % MOE_HID_TILE == 0
    assert n_pages % PAGES_PER_STEP == 0 and n_pages // PAGES_PER_STEP > 1

    xp = x_prompt.reshape(mp, d)
    xs = x_sample.reshape(ms, d)

    n_c = bp + bs
    pad = (-n_c) % BF16_ROWS
    c_all = jnp.concatenate([c_prompt, c_sample, jnp.zeros((pad, d), F32)], axis=0)
    mod = adaln_all(c_all, ada_w, ada_b)

    lb_prob = jax.nn.softmax(hgrn_lower_bound.astype(F32), axis=0)
    lower_bounds = jnp.cumsum(lb_prob, axis=0) - lb_prob[0]

    tile = min(ATT_TILE, lp)
    ar = jnp.arange(tile)
    near = [_toeplitz(_bias_by_distance(qoff + jnp.arange(3 * tile) - (2 * tile - 1), rel_bias_table), 2 * tile, tile)
            for qoff in (0, tile)]
    bias_near = jnp.stack([jnp.concatenate([b, b], axis=2) for b in near], axis=1)
    bias_far = rel_bias_table[_bucket(jnp.array(MAX_DISTANCE, jnp.int32))].astype(F32)
    pos_s = past_len + jnp.arange(ls)
    qrows = DF_HEADS * 2 * ls

    def dec_bias(b_hqk):
        hq = jnp.stack([b_hqk, b_hqk], axis=1).reshape(qrows, b_hqk.shape[-1])
        row_head = jnp.arange(qrows) // (2 * ls) % SUBLANES
        own = row_head[:, None] == jnp.arange(SUBLANES)[None, :]
        return jnp.where(own[:, None, :], hq[:, :, None], NEG_BIG).reshape(qrows, -1)

    far_hqk = jnp.broadcast_to(bias_far[:, None, None], (DF_HEADS, ls, page))
    dbias_far = dec_bias(far_hqk)
    dbias_last = dec_bias(_bias_tile(pos_s, past_len - page + jnp.arange(page), rel_bias_table))
    dbias_tail = jnp.stack([dbias_far, dbias_last])
    new_kpos = jnp.where(jnp.arange(page) < ls, past_len + jnp.arange(page), past_len + 2 * page)
    dbias_new = dec_bias(_bias_tile(pos_s, new_kpos, rel_bias_table))

    outs = {k: [] for k in ('k_p', 'v_p', 'k_s', 'v_s', 'st_p', 'st_s')}
    for i in range(depth):
        j = i // N_MIXERS
        mods = jnp.split(mod[i], 6, axis=-1)
        mp_ = [m[:bp].reshape(bp, 1, d) for m in mods]
        ms_ = [_rows(m[bp:bp + bs], ls) for m in mods]
        g_mix = norm_mix[i].reshape(1, d)
        g_ffn = norm_ffn[i].reshape(1, d)

        if i % N_MIXERS == 0:
            w_in = hgrn_w_in[j].astype(BF16)
            w_out = hgrn_w_out[j].astype(BF16)
            lb = lower_bounds[j].reshape(1, -1)
            onorm = hgrn_out_norm[j].astype(F32).reshape(1, -1)
            dk = onorm.shape[1]
            qfig_p = norm_mod_matmul(xp, g_mix, mp_[0], mp_[1], w_in, lp, ROW_TILE)
            qfig_s = norm_mod_matmul(xs, g_mix, ms_[0], ms_[1], w_in, ls, ms)
            s0_p = jnp.zeros((bp, HG_HEADS, dk, dk), F32)
            op, sp = hgrn_recurrence(qfig_p, lb, onorm, s0_p, bp, lp, BF16)
            os_, ss = hgrn_recurrence(qfig_s, lb, onorm, state_hgrn[j].astype(F32), bs, ls, F32)
            os_ = os_.astype(BF16)
            outs['st_p'].append(sp.astype(state_hgrn.dtype))
            outs['st_s'].append(ss.astype(state_hgrn.dtype))
        else:
            lam_init = 0.8 - 0.6 * math.exp(-0.3 * i)
            lpf = diff_lambda[j].astype(F32)
            lam = (jnp.exp(jnp.sum(lpf[0] * lpf[1])) - jnp.exp(jnp.sum(lpf[2] * lpf[3])) + lam_init).reshape(1)
            w_in = diff_w_in[j].astype(BF16)
            w_out = diff_w_out[j].astype(BF16)
            dh = diff_qk_norm.shape[-1]
            qn = jnp.tile(diff_qk_norm[j, 0], 2).reshape(1, 2 * dh) * (dh ** -0.5)
            kn = jnp.tile(diff_qk_norm[j, 1], 2).reshape(1, 2 * dh)
            subln = diff_subln[j].reshape(1, -1)
            q_p, k_p, v_p = norm_mod_qkv(xp, g_mix, mp_[0], mp_[1], w_in, qn, kn, lp, ROW_TILE)
            q_s, k_s, v_s = norm_mod_qkv(xs, g_mix, ms_[0], ms_[1], w_in, qn, kn, ls, ms)
            op = attn_prompt(q_p, k_p, v_p, lam, bias_far, bias_near, subln, bp, lp, 1.0 - lam_init)
            q4 = q_s.reshape(bs, ls, DF_HEADS, 2 * dh).transpose(0, 2, 1, 3)
            lo = jnp.arange(2 * dh) < dh
            qz = jnp.stack([jnp.where(lo, q4, 0), jnp.where(lo, 0, q4)], axis=2).reshape(bs, DF_HEADS * 2 * ls, 2 * dh)
            k_s5 = k_s.reshape(bs, ls, DF_HEADS, 2 * dh)
            v_s5 = v_s.reshape(bs, ls, DF_HEADS, 2 * dh)
            padw = ((0, 0), (0, page - ls), (0, 0), (0, 0))
            o_s = attn_decode(qz, cache_k, cache_v, j, jnp.pad(k_s5, padw), jnp.pad(v_s5, padw), page_table, lam,
                              dbias_far, dbias_tail, dbias_new, subln, 1.0 - lam_init)
            os_ = o_s.transpose(0, 2, 1, 3).reshape(ms, DF_HEADS * 2 * dh).astype(BF16)
            outs['k_p'].append(to_heads(k_p, DF_HEADS, COMBINE_TILE).reshape(bp, lp, DF_HEADS, 2 * dh))
            outs['v_p'].append(to_heads(v_p, DF_HEADS, COMBINE_TILE).reshape(bp, lp, DF_HEADS, 2 * dh))
            outs['k_s'].append(k_s5)
            outs['v_s'].append(v_s5)

        xp = matmul_residual(op, w_out, xp, mp_[2], lp, ROW_TILE)
        xs = matmul_residual(os_, w_out, xs, ms_[2], ls, ms)

        nr = LANES
        wr = jnp.concatenate([router_group_w[i], router_expert_w[i],
                              jnp.zeros((d, nr - N_GROUPS * (1 + EXPERTS_PER_GROUP)), F32)], axis=1)
        wr_hi = wr.astype(BF16)
        wr_lo = (wr - wr_hi.astype(F32)).astype(BF16)
        h_p, lg_p = norm_mod_router(xp, g_ffn, mp_[3], mp_[4], wr_hi, wr_lo, lp, ROW_TILE, mp + ms)
        h_s, lg_s = norm_mod_router(xs, g_ffn, ms_[3], ms_[4], wr_hi, wr_lo, ls, ms, ms)
        h_all = lax.dynamic_update_slice(h_p, h_s, (mp, 0))
        lg_all = lax.dynamic_update_slice(lg_p, lg_s, (mp, 0))
        expert, gate = route(lg_all, router_group_b[i], router_expert_b[i])
        n_hid_tiles = expert_w1.shape[-1] // MOE_HID_TILE
        dest, slot_tok, table = moe_plan(expert, n_experts, n_hid_tiles)
        yb = moe_experts(h_all[slot_tok], table, expert_w1, expert_w3, expert_w2, i)
        dest2 = dest.reshape(mp + ms, TOP_K_IN_GROUP)
        y0, y1 = yb[dest2[:, 0]], yb[dest2[:, 1]]
        g0, g1 = gate[:, 0:1], gate[:, 1:2]
        xp = moe_combine(xp, mp_[5], y0, y1, g0, g1, 0, lp, COMBINE_TILE)
        xs = moe_combine(xs, ms_[5], y0, y1, g0, g1, mp, ls, ms)

    return (xp.reshape(bp, lp, d), xs.reshape(bs, ls, d),
            jnp.stack(outs['k_p']), jnp.stack(outs['v_p']), jnp.stack(outs['k_s']), jnp.stack(outs['v_s']),
            jnp.stack(outs['st_p']), jnp.stack(outs['st_s']))
```

```python
import functools
import math

import jax
import jax.numpy as jnp
from jax import lax
from jax.experimental import pallas as pl
from jax.experimental.pallas import tpu as pltpu

F32 = jnp.float32
BF16 = jnp.bfloat16

N_MIXERS = 2
HG_HEADS = 16
DF_HEADS = 16
NUM_BUCKETS = 32
MAX_DISTANCE = 128
N_GROUPS = 4
EXPERTS_PER_GROUP = 8
TOP_K_IN_GROUP = 2
EPS = 1e-6
NEG_BIG = -1e30

LANES = 128
SUBLANES = 8
BF16_ROWS = 16
VMEM_LIMIT = 56 * 1024 * 1024

ROW_TILE = 1024
COL_TILE = 512
COMBINE_TILE = 512
HG_CHUNK = 64
HG_SUB = 16
HG_HEADS_PER_STEP = 4
ATT_HEADS_PER_STEP = 2
ATT_TILE = 256
PAGES_PER_STEP = 4
MOE_BLOCK = 256
MOE_HID_TILE = 512
MOE_CHUNK_BLOCKS = 4

NT_DIMS = (((1,), (1,)), ((), ()))
TN_DIMS = (((0,), (0,)), ((), ()))


def _cparams(sem):
    return pltpu.CompilerParams(dimension_semantics=sem, vmem_limit_bytes=VMEM_LIMIT)


def _silu(x):
    return x * jax.nn.sigmoid(x)


def _norm_mod(x, g, shift, scale):
    y = x * lax.rsqrt(jnp.mean(x * x, axis=-1, keepdims=True) + EPS)
    return (y * g) * (1.0 + scale) + shift


def _split3(x):
    x1 = x.astype(BF16)
    r1 = x - x1.astype(F32)
    x2 = r1.astype(BF16)
    x3 = (r1 - x2.astype(F32)).astype(BF16)
    return x1, x2, x3


def _adaln_kernel(c_ref, w_ref, b_ref, o_ref):
    a = _silu(c_ref[...]).astype(BF16)
    o_ref[...] = jnp.dot(a, w_ref[...].astype(BF16), preferred_element_type=F32) + b_ref[...]


def adaln_all(c_all, ada_w, ada_b, tn=1024):
    depth, d, n = ada_w.shape
    r = c_all.shape[0]
    return pl.pallas_call(
        _adaln_kernel,
        grid=(depth, n // tn),
        in_specs=[pl.BlockSpec((r, d), lambda l, j: (0, 0)),
                  pl.BlockSpec((None, d, tn), lambda l, j: (l, 0, j)),
                  pl.BlockSpec((None, 1, tn), lambda l, j: (l, 0, j))],
        out_specs=pl.BlockSpec((None, r, tn), lambda l, j: (l, 0, j)),
        out_shape=jax.ShapeDtypeStruct((depth, r, n), F32),
        compiler_params=_cparams(("arbitrary", "arbitrary")),
        name="adaln",
    )(c_all, ada_w, ada_b.reshape(depth, 1, n))


def _mod_spec(a, tm, rows_per_seq):
    if a.ndim == 3:
        blocks_per_seq = rows_per_seq // tm
        return pl.BlockSpec((None, 1, a.shape[-1]), lambda i, j: (i // blocks_per_seq, 0, 0))
    return pl.BlockSpec((tm, a.shape[-1]), lambda i, j: (i, 0))


def _mod_spec_cols(a, tm, tn, rows_per_seq):
    if a.ndim == 3:
        blocks_per_seq = rows_per_seq // tm
        return pl.BlockSpec((None, 1, tn), lambda i, j: (i // blocks_per_seq, 0, j))
    return pl.BlockSpec((tm, tn), lambda i, j: (i, j))


def _nmm_plain_kernel(x_ref, g_ref, sh_ref, sc_ref, w_ref, o_ref, h_scr):
    @pl.when(pl.program_id(1) == 0)
    def _():
        h_scr[...] = _norm_mod(x_ref[...], g_ref[...], sh_ref[...], sc_ref[...]).astype(BF16)

    o_ref[...] = jnp.dot(h_scr[...], w_ref[...], preferred_element_type=F32)


def _qk_norm_store(acc, wn, out_ref):
    tn = acc.shape[1]
    lo_mask = lax.broadcasted_iota(jnp.int32, (1, LANES), 1) < (LANES // 2)
    for gi in range(tn // LANES):
        s = acc[:, gi * LANES:(gi + 1) * LANES]
        sq = s * s
        lo = jnp.sum(jnp.where(lo_mask, sq, 0.0), axis=-1, keepdims=True)
        hi = jnp.sum(jnp.where(lo_mask, 0.0, sq), axis=-1, keepdims=True)
        ms = jnp.where(lo_mask, lo, hi) * (2.0 / LANES)
        out_ref[:, gi * LANES:(gi + 1) * LANES] = (s * lax.rsqrt(ms + EPS) * wn).astype(out_ref.dtype)


def _nmm_qkv_kernel(x_ref, g_ref, sh_ref, sc_ref, w_ref, qn_ref, kn_ref, q_ref, k_ref, v_ref, h_scr, *, nt):
    j = pl.program_id(1)

    @pl.when(j == 0)
    def _():
        h_scr[...] = _norm_mod(x_ref[...], g_ref[...], sh_ref[...], sc_ref[...]).astype(BF16)

    acc = jnp.dot(h_scr[...], w_ref[...], preferred_element_type=F32)

    @pl.when(j < nt)
    def _():
        _qk_norm_store(acc, qn_ref[...], q_ref)

    @pl.when((j >= nt) & (j < 2 * nt))
    def _():
        _qk_norm_store(acc, kn_ref[...], k_ref)

    @pl.when(j >= 2 * nt)
    def _():
        v_ref[...] = acc


def norm_mod_matmul(x, g, shift, scale, w, rows_per_seq, tm, tn=COL_TILE):
    m, d = x.shape
    n = w.shape[1]
    return pl.pallas_call(
        _nmm_plain_kernel,
        grid=(m // tm, n // tn),
        in_specs=[pl.BlockSpec((tm, d), lambda i, j: (i, 0)),
                  pl.BlockSpec((1, d), lambda i, j: (0, 0)),
                  _mod_spec(shift, tm, rows_per_seq),
                  _mod_spec(scale, tm, rows_per_seq),
                  pl.BlockSpec((d, tn), lambda i, j: (0, j))],
        out_specs=pl.BlockSpec((tm, tn), lambda i, j: (i, j)),
        out_shape=jax.ShapeDtypeStruct((m, n), F32),
        scratch_shapes=[pltpu.VMEM((tm, d), BF16)],
        compiler_params=_cparams(("arbitrary", "arbitrary")),
        name="norm_proj",
    )(x, g, shift, scale, w)


def norm_mod_qkv(x, g, shift, scale, w, qn, kn, rows_per_seq, tm, tn=COL_TILE):
    m, d = x.shape
    part = w.shape[1] // 3
    nt = part // tn
    clamp = lambda j, lo: jnp.clip(j - lo, 0, nt - 1)
    return pl.pallas_call(
        functools.partial(_nmm_qkv_kernel, nt=nt),
        grid=(m // tm, 3 * nt),
        in_specs=[pl.BlockSpec((tm, d), lambda i, j: (i, 0)),
                  pl.BlockSpec((1, d), lambda i, j: (0, 0)),
                  _mod_spec(shift, tm, rows_per_seq),
                  _mod_spec(scale, tm, rows_per_seq),
                  pl.BlockSpec((d, tn), lambda i, j: (0, j)),
                  pl.BlockSpec((1, LANES), lambda i, j: (0, 0)),
                  pl.BlockSpec((1, LANES), lambda i, j: (0, 0))],
        out_specs=[pl.BlockSpec((tm, tn), lambda i, j: (i, clamp(j, 0))),
                   pl.BlockSpec((tm, tn), lambda i, j: (i, clamp(j, nt))),
                   pl.BlockSpec((tm, tn), lambda i, j: (i, clamp(j, 2 * nt)))],
        out_shape=[jax.ShapeDtypeStruct((m, part), BF16),
                   jax.ShapeDtypeStruct((m, part), F32),
                   jax.ShapeDtypeStruct((m, part), F32)],
        scratch_shapes=[pltpu.VMEM((tm, d), BF16)],
        compiler_params=_cparams(("arbitrary", "arbitrary")),
        name="norm_qkv",
    )(x, g, shift, scale, w, qn, kn)


def _mmres_kernel(o_ref, w_ref, x_ref, ga_ref, out_ref):
    out_ref[...] = x_ref[...] + ga_ref[...] * jnp.dot(o_ref[...], w_ref[...], preferred_element_type=F32)


def matmul_residual(o, w, x, gate, rows_per_seq, tm, tn=COL_TILE):
    m, kdim = o.shape
    n = w.shape[1]
    return pl.pallas_call(
        _mmres_kernel,
        grid=(m // tm, n // tn),
        in_specs=[pl.BlockSpec((tm, kdim), lambda i, j: (i, 0)),
                  pl.BlockSpec((kdim, tn), lambda i, j: (0, j)),
                  pl.BlockSpec((tm, tn), lambda i, j: (i, j)),
                  _mod_spec_cols(gate, tm, tn, rows_per_seq)],
        out_specs=pl.BlockSpec((tm, tn), lambda i, j: (i, j)),
        out_shape=jax.ShapeDtypeStruct((m, n), F32),
        compiler_params=_cparams(("arbitrary", "arbitrary")),
        name="out_proj_residual",
    )(o, w, x, gate)


def _combine_kernel(x_ref, ga_ref, y0_ref, y1_ref, g0_ref, g1_ref, o_ref):
    o_ref[...] = x_ref[...] + ga_ref[...] * (g0_ref[...] * y0_ref[...] + g1_ref[...] * y1_ref[...])


def moe_combine(x, gate_mod, y0, y1, g0, g1, row0, rows_per_seq, tm):
    m, d = x.shape
    b0 = row0 // tm
    tok = lambda i, j: (b0 + i, 0)
    return pl.pallas_call(
        _combine_kernel,
        grid=(m // tm, 1),
        in_specs=[pl.BlockSpec((tm, d), lambda i, j: (i, 0)),
                  _mod_spec_cols(gate_mod, tm, d, rows_per_seq),
                  pl.BlockSpec((tm, d), tok), pl.BlockSpec((tm, d), tok),
                  pl.BlockSpec((tm, 1), tok), pl.BlockSpec((tm, 1), tok)],
        out_specs=pl.BlockSpec((tm, d), lambda i, j: (i, 0)),
        out_shape=jax.ShapeDtypeStruct((m, d), F32),
        compiler_params=_cparams(("arbitrary", "arbitrary")),
        name="moe_combine",
    )(x, gate_mod, y0, y1, g0, g1)


def _to_heads_kernel(x_ref, o_ref):
    for h in range(o_ref.shape[1]):
        o_ref[:, h, :] = x_ref[:, h * LANES:(h + 1) * LANES]


def to_heads(x, heads, tm):
    m, width = x.shape
    return pl.pallas_call(
        _to_heads_kernel,
        grid=(m // tm,),
        in_specs=[pl.BlockSpec((tm, width), lambda i: (i, 0))],
        out_specs=pl.BlockSpec((tm, heads, LANES), lambda i: (i, 0, 0)),
        out_shape=jax.ShapeDtypeStruct((m, heads, LANES), x.dtype),
        compiler_params=_cparams(("arbitrary",)),
        name="to_heads",
    )(x)


def _nrouter_kernel(x_ref, g_ref, sh_ref, sc_ref, wh_ref, wl_ref, h_ref, lg_ref, *, n_in_blocks):
    i = pl.program_id(0)

    @pl.when(i < n_in_blocks)
    def _():
        h = _norm_mod(x_ref[...], g_ref[...], sh_ref[...], sc_ref[...])
        h_ref[...] = h
        h1 = h.astype(BF16)
        h2 = (h - h1.astype(F32)).astype(BF16)
        wh = wh_ref[...]
        lg_ref[...] = (jnp.dot(h1, wh, preferred_element_type=F32)
                       + jnp.dot(h1, wl_ref[...], preferred_element_type=F32)
                       + jnp.dot(h2, wh, preferred_element_type=F32))

    @pl.when(i >= n_in_blocks)
    def _():
        h_ref[...] = jnp.zeros(h_ref.shape, h_ref.dtype)
        lg_ref[...] = jnp.zeros(lg_ref.shape, lg_ref.dtype)


def norm_mod_router(x, g, shift, scale, wr_hi, wr_lo, rows_per_seq, tm, out_rows):
    m, d = x.shape
    nr = wr_hi.shape[1]
    nb = m // tm
    clamp = lambda f: (lambda i: f(jnp.minimum(i, nb - 1), 0))
    sh_spec = _mod_spec(shift, tm, rows_per_seq)
    sc_spec = _mod_spec(scale, tm, rows_per_seq)
    return pl.pallas_call(
        functools.partial(_nrouter_kernel, n_in_blocks=nb),
        grid=(pl.cdiv(out_rows, tm),),
        in_specs=[pl.BlockSpec((tm, d), clamp(lambda i, j: (i, 0))),
                  pl.BlockSpec((1, d), lambda i: (0, 0)),
                  pl.BlockSpec(sh_spec.block_shape, clamp(sh_spec.index_map)),
                  pl.BlockSpec(sc_spec.block_shape, clamp(sc_spec.index_map)),
                  pl.BlockSpec((d, nr), lambda i: (0, 0)),
                  pl.BlockSpec((d, nr), lambda i: (0, 0))],
        out_specs=[pl.BlockSpec((tm, d), lambda i: (i, 0)),
                   pl.BlockSpec((tm, nr), lambda i: (i, 0))],
        out_shape=[jax.ShapeDtypeStruct((out_rows, d), F32),
                   jax.ShapeDtypeStruct((out_rows, nr), F32)],
        compiler_params=_cparams(("arbitrary",)),
        name="norm_router",
    )(x, g, shift, scale, wr_hi, wr_lo)


def _chunk_cumsum(x, tri):
    c = x.shape[0]
    if tri is None:
        row = lax.broadcasted_iota(jnp.int32, x.shape, 0)
        out = jnp.zeros_like(x)
        for s in range(c):
            out = out + jnp.where(row >= s, x[s:s + 1], 0.0)
        return out
    x1, x2, x3 = _split3(x)
    return (jnp.dot(tri, x1, preferred_element_type=F32)
            + jnp.dot(tri, x2, preferred_element_type=F32)
            + jnp.dot(tri, x3, preferred_element_type=F32))


def _hgrn_kernel(q_ref, f_ref, i_ref, g_ref, lb_ref, on_ref, s0_ref, o_ref, sn_ref, st_scr, *, seq, chunk, sub, hb):
    lb = lb_ref[...]
    on = on_ref[...]
    for hh in range(hb):
        st_scr[hh] = s0_ref[hh].T
    nsub = chunk // sub
    row_sub = lax.broadcasted_iota(jnp.int32, (sub, LANES), 0)
    if chunk > BF16_ROWS:
        tri = (lax.broadcasted_iota(jnp.int32, (chunk, chunk), 0)
               >= lax.broadcasted_iota(jnp.int32, (chunk, chunk), 1)).astype(BF16)
    else:
        tri = None

    def chunk_body(ci, carry):
        r0 = pl.multiple_of(ci * chunk, chunk)
        q = q_ref[pl.ds(r0, chunk), :]
        f = f_ref[pl.ds(r0, chunk), :]
        v = i_ref[pl.ds(r0, chunk), :]
        g = g_ref[pl.ds(r0, chunk), :]
        qs_all = _silu(q)
        log_f = jnp.log(lb + (1.0 - lb) * jax.nn.sigmoid(f))
        k_all = (1.0 - lb) * jax.nn.sigmoid(-f)
        b_all = _chunk_cumsum(log_f, tri)
        sg_all = _silu(g)
        outs = []
        for hh in range(hb):
            cs = slice(hh * LANES, (hh + 1) * LANES)
            qs, k, b, vh = qs_all[:, cs], k_all[:, cs], b_all[:, cs], v[:, cs]
            st = st_scr[hh]
            vb = vh.astype(BF16)
            o_inter = lax.dot_general((qs * jnp.exp(b)).astype(BF16), st.astype(BF16), NT_DIMS,
                                      preferred_element_type=F32)
            parts = []
            for blk in range(nsub):
                lo = blk * sub
                q_b, b_b, k_b, v_b = qs[lo:lo + sub], b[lo:lo + sub], k[lo:lo + sub], vh[lo:lo + sub]
                o_b = o_inter[lo:lo + sub]
                if blk > 0:
                    b_ref_row = b[lo - 1:lo]
                    qd = (q_b * jnp.exp(b_b - b_ref_row)).astype(BF16)
                    kd = (k[:lo] * jnp.exp(b_ref_row - b[:lo])).astype(BF16)
                    sc = lax.dot_general(qd, kd, NT_DIMS, preferred_element_type=F32)
                    o_b = o_b + jnp.dot(sc.astype(BF16), vb[:lo], preferred_element_type=F32)
                for s in range(sub):
                    e = jnp.exp(b_b - b_b[s:s + 1])
                    a = jnp.where(row_sub >= s, q_b * e * k_b[s:s + 1], 0.0)
                    o_b = o_b + jnp.sum(a, axis=-1, keepdims=True) * v_b[s:s + 1]
                parts.append(o_b)
            o = parts[0] if nsub == 1 else jnp.concatenate(parts, axis=0)
            b_last = b[chunk - 1:chunk]
            kdc = k * jnp.exp(b_last - b)
            k1 = kdc.astype(BF16)
            k2 = (kdc - k1.astype(F32)).astype(BF16)
            v2 = (vh - vb.astype(F32)).astype(BF16)
            upd = (lax.dot_general(vb, k1, TN_DIMS, preferred_element_type=F32)
                   + lax.dot_general(vb, k2, TN_DIMS, preferred_element_type=F32)
                   + lax.dot_general(v2, k1, TN_DIMS, preferred_element_type=F32))
            st_scr[hh] = st * jnp.exp(b_last) + upd
            outs.append(o * lax.rsqrt(jnp.mean(o * o, axis=-1, keepdims=True) + EPS) * on * sg_all[:, cs])
        y = outs[0] if hb == 1 else jnp.concatenate(outs, axis=1)
        o_ref[pl.ds(r0, chunk), :] = y.astype(o_ref.dtype)
        return carry

    lax.fori_loop(0, seq // chunk, chunk_body, 0)
    for hh in range(hb):
        sn_ref[hh] = st_scr[hh].T


def hgrn_recurrence(qfig, lb, out_norm, s0, batch, seq, out_dtype):
    heads = s0.shape[1]
    dk = s0.shape[2]
    hb = HG_HEADS_PER_STEP
    ng = heads // hb
    chunk = min(HG_CHUNK, seq)
    sub = min(HG_SUB, chunk)
    col = lambda part: (lambda b, h: (b, part * ng + h))
    return pl.pallas_call(
        functools.partial(_hgrn_kernel, seq=seq, chunk=chunk, sub=sub, hb=hb),
        grid=(batch, ng),
        in_specs=[pl.BlockSpec((seq, hb * dk), col(0)),
                  pl.BlockSpec((seq, hb * dk), col(1)),
                  pl.BlockSpec((seq, hb * dk), col(2)),
                  pl.BlockSpec((seq, hb * dk), col(3)),
                  pl.BlockSpec((1, hb * dk), lambda b, h: (0, h)),
                  pl.BlockSpec((1, dk), lambda b, h: (0, 0)),
                  pl.BlockSpec((None, hb, dk, dk), lambda b, h: (b, h, 0, 0))],
        out_specs=[pl.BlockSpec((seq, hb * dk), lambda b, h: (b, h)),
                   pl.BlockSpec((None, hb, dk, dk), lambda b, h: (b, h, 0, 0))],
        out_shape=[jax.ShapeDtypeStruct((batch * seq, heads * dk), out_dtype),
                   jax.ShapeDtypeStruct(s0.shape, F32)],
        scratch_shapes=[pltpu.VMEM((hb, dk, dk), F32)],
        compiler_params=_cparams(("arbitrary", "arbitrary")),
        name="hgrn_recurrence",
    )(qfig, qfig, qfig, qfig, lb, out_norm, s0)


def _bucket(n):
    max_exact = NUM_BUCKETS // 2
    large = max_exact + (jnp.log(jnp.maximum(n, max_exact).astype(F32) / max_exact)
                         / math.log(MAX_DISTANCE / max_exact) * (NUM_BUCKETS - max_exact)).astype(jnp.int32)
    return jnp.where(n < max_exact, n, jnp.minimum(large, NUM_BUCKETS - 1))


def _bias_tile(qpos, kpos, table):
    n = jnp.maximum(qpos[:, None] - kpos[None, :], 0)
    bias = jnp.moveaxis(table[_bucket(n)], -1, 0).astype(F32)
    return jnp.where((qpos[:, None] >= kpos[None, :])[None], bias, NEG_BIG)


def _bias_by_distance(n, table):
    b = table[_bucket(jnp.maximum(n, 0))].astype(F32).T
    return jnp.where(n[None, :] >= 0, b, NEG_BIG)


def _toeplitz(gvec, nrow, ncol):
    h, p = gvec.shape
    skew = jnp.tile(gvec, (1, nrow))[:, :nrow * (p - 1)].reshape(h, nrow, p - 1)
    return skew[:, :, nrow - 1:nrow - 1 + ncol]


def _attn_prompt_kernel(lam_ref, bc_ref, q_ref, k_ref, v_ref, bn_ref, sub_ref, o_ref, kb, vb,
                        *, tile, hb, out_scale):
    hg = pl.program_id(1)
    qi = pl.program_id(2)

    cols = [slice(hh * LANES, (hh + 1) * LANES) for hh in range(hb)]

    @pl.when(qi == 0)
    def _():
        kb[...] = k_ref[...].astype(BF16)
        for hh in range(hb):
            vb[hh] = v_ref[:, cols[hh]].T.astype(BF16)

    q = q_ref[...].astype(F32)
    lo_mask = lax.broadcasted_iota(jnp.int32, (LANES, tile), 0) < (LANES // 2)
    qqs = []
    for cs in cols:
        qt = q[:, cs].T
        qqs.append(jnp.concatenate([jnp.where(lo_mask, qt, 0.0), jnp.where(lo_mask, 0.0, qt)], axis=1).astype(BF16))

    nch = 2 * tile // LANES

    def step(qq, kt, vt, bias, carry):
        out = []
        for c in range(nch):
            m, l, acc = carry[c]
            b = bias(c) if callable(bias) else bias
            s = jnp.dot(kt, qq[:, c * LANES:(c + 1) * LANES], preferred_element_type=F32) + b
            m_new = jnp.maximum(m, jnp.max(s, axis=0, keepdims=True))
            alpha = jnp.exp(m - m_new)
            p = jnp.exp(s - m_new)
            l = l * alpha + jnp.sum(p, axis=0, keepdims=True)
            acc = acc * alpha + jnp.dot(vt, p.astype(BF16), preferred_element_type=F32)
            out.append((m_new, l, acc))
        return tuple(out)

    def far_body(kv, carry):
        r0 = pl.multiple_of(kv * tile, tile)
        return tuple(step(qqs[hh], kb[pl.ds(r0, tile), cols[hh]], vb[hh, :, pl.ds(r0, tile)],
                          bc_ref[hg * hb + hh], carry[hh]) for hh in range(hb))

    init = tuple(tuple((jnp.full((1, LANES), NEG_BIG, F32), jnp.zeros((1, LANES), F32),
                        jnp.zeros((LANES, LANES), F32)) for _ in range(nch)) for _ in range(hb))
    n_far = jnp.maximum(qi - 1, 0)
    carry = lax.fori_loop(0, n_far, far_body, init)
    n0 = pl.multiple_of(n_far * tile, tile)
    outs = []
    for hh in range(hb):
        fin = step(qqs[hh], kb[pl.ds(n0, 2 * tile), cols[hh]], vb[hh, :, pl.ds(n0, 2 * tile)],
                   lambda c, hh=hh: bn_ref[hh, :, c * LANES:(c + 1) * LANES], carry[hh])
        o = jnp.concatenate([acc / l for _, l, acc in fin], axis=1)
        od = o[:, :tile] - lam_ref[0] * o[:, tile:]
        yt = od * lax.rsqrt(jnp.mean(od * od, axis=0, keepdims=True) + EPS)
        outs.append(yt.T * sub_ref[...] * out_scale)
    y = outs[0] if hb == 1 else jnp.concatenate(outs, axis=1)
    o_ref[...] = y.astype(o_ref.dtype)


def attn_prompt(q, k, v, lam, bias_far, bias_near, subln, batch, seq, out_scale):
    heads = q.shape[1] // LANES
    hb = ATT_HEADS_PER_STEP
    tile = min(ATT_TILE, seq)
    nq = seq // tile
    smem = pl.BlockSpec(memory_space=pltpu.SMEM)
    return pl.pallas_call(
        functools.partial(_attn_prompt_kernel, tile=tile, hb=hb, out_scale=out_scale),
        grid=(batch, heads // hb, nq),
        in_specs=[smem, smem,
                  pl.BlockSpec((tile, hb * LANES), lambda b, h, i: (b * nq + i, h)),
                  pl.BlockSpec((seq, hb * LANES), lambda b, h, i: (b, h)),
                  pl.BlockSpec((seq, hb * LANES), lambda b, h, i: (b, h)),
                  pl.BlockSpec((hb, None, 2 * tile, 2 * tile), lambda b, h, i: (h, jnp.minimum(i, 1), 0, 0)),
                  pl.BlockSpec((1, LANES), lambda b, h, i: (0, 0))],
        out_specs=pl.BlockSpec((tile, hb * LANES), lambda b, h, i: (b * nq + i, h)),
        out_shape=jax.ShapeDtypeStruct(q.shape, BF16),
        scratch_shapes=[pltpu.VMEM((seq, hb * LANES), BF16), pltpu.VMEM((hb, LANES, seq), BF16)],
        compiler_params=_cparams(("arbitrary", "arbitrary", "arbitrary")),
        name="attn_prompt",
    )(lam, bias_far, q, k, v, bias_near, subln)


def _attn_decode_kernel(pt_ref, lam_ref, q_ref, *refs, npg, heads, out_scale):
    k_refs = refs[:npg]
    v_refs = refs[npg:2 * npg]
    kn_ref, vn_ref, bf_ref, bt_ref, bn_ref, sub_ref, o_ref, m_scr, l_scr, acc_scr = refs[2 * npg:]
    p = pl.program_id(1)
    ngrp = heads // SUBLANES
    grows = q_ref.shape[0] // ngrp
    dh = q_ref.shape[1]

    @pl.when(p == 0)
    def _():
        m_scr[...] = jnp.full(m_scr.shape, NEG_BIG, F32)
        l_scr[...] = jnp.zeros(l_scr.shape, F32)
        acc_scr[...] = jnp.zeros(acc_scr.shape, F32)

    q = q_ref[...]

    def pages(k_list, v_list, b_list):
        for a in range(ngrp):
            rs = slice(a * grows, (a + 1) * grows)
            hs = slice(a * SUBLANES, (a + 1) * SUBLANES)
            ss = [lax.dot_general(q[rs], k_ref[:, hs, :].reshape(-1, dh).astype(BF16), NT_DIMS,
                                  preferred_element_type=F32) + b_ref[rs, :]
                  for k_ref, b_ref in zip(k_list, b_list)]
            m_old = m_scr[rs, :]
            m_new = m_old
            for s in ss:
                m_new = jnp.maximum(m_new, jnp.max(s, axis=-1, keepdims=True))
            alpha = jnp.exp(m_old - m_new)
            l_new = l_scr[rs, :] * alpha
            acc_new = acc_scr[rs, :] * alpha
            for s, v_ref in zip(ss, v_list):
                pe = jnp.exp(s - m_new)
                l_new = l_new + jnp.sum(pe, axis=-1, keepdims=True)
                acc_new = acc_new + jnp.dot(pe.astype(BF16), v_ref[:, hs, :].reshape(-1, dh).astype(BF16),
                                            preferred_element_type=F32)
            l_scr[rs, :] = l_new
            acc_scr[rs, :] = acc_new
            m_scr[rs, :] = m_new

    pages(k_refs, v_refs, [bf_ref] * (npg - 1) + [bt_ref])

    @pl.when(p == pl.num_programs(1) - 1)
    def _():
        pages([kn_ref], [vn_ref], [bn_ref])
        o = acc_scr[...] / l_scr[...]
        rows = q_ref.shape[0] // heads
        o4 = o.reshape(heads, 2, rows // 2, LANES)
        od = o4[:, 0] - lam_ref[0] * o4[:, 1]
        y = od * lax.rsqrt(jnp.mean(od * od, axis=-1, keepdims=True) + EPS) * sub_ref[...] * out_scale
        o_ref[...] = y


def attn_decode(qz, cache_k, cache_v, layer, k_new, v_new, page_table, lam, bias_far, bias_tail, bias_new,
                subln, out_scale):
    bd, qrows, _ = qz.shape
    _, _, page, heads, dh = cache_k.shape
    n_pages = page_table.shape[1]
    npg = PAGES_PER_STEP
    nsteps = n_pages // npg
    nq = qrows // heads // 2
    bcols = page * SUBLANES
    page_spec = lambda gi: pl.BlockSpec(
        (None, None, page, heads, dh), lambda b, p, pt: (layer, pt[b, p * npg + gi], 0, 0, 0))
    const2 = lambda b, p, pt: (0, 0)
    new_spec = pl.BlockSpec((None, k_new.shape[1], heads, dh), lambda b, p, pt: (b, 0, 0, 0))
    grid_spec = pltpu.PrefetchScalarGridSpec(
        num_scalar_prefetch=1,
        grid=(bd, nsteps),
        in_specs=[pl.BlockSpec(memory_space=pltpu.SMEM),
                  pl.BlockSpec((None, qrows, dh), lambda b, p, pt: (b, 0, 0))]
                 + [page_spec(gi) for gi in range(npg)] + [page_spec(gi) for gi in range(npg)]
                 + [new_spec, new_spec,
                    pl.BlockSpec((qrows, bcols), const2),
                    pl.BlockSpec((None, qrows, bcols), lambda b, p, pt: (p // (nsteps - 1), 0, 0)),
                    pl.BlockSpec((qrows, bias_new.shape[1]), const2), pl.BlockSpec((1, dh), const2)],
        out_specs=pl.BlockSpec((None, heads, nq, dh), lambda b, p, pt: (b, 0, 0, 0)),
        scratch_shapes=[pltpu.VMEM((qrows, 1), F32), pltpu.VMEM((qrows, 1), F32), pltpu.VMEM((qrows, dh), F32)],
    )
    return pl.pallas_call(
        functools.partial(_attn_decode_kernel, npg=npg, heads=heads, out_scale=out_scale),
        grid_spec=grid_spec,
        out_shape=jax.ShapeDtypeStruct((bd, heads, nq, dh), F32),
        compiler_params=_cparams(("arbitrary", "arbitrary")),
        name="attn_decode",
    )(page_table, lam, qz, *([cache_k] * npg), *([cache_v] * npg), k_new, v_new,
      bias_far, bias_tail, bias_new, subln)


def route(logits, bg, be):
    t = logits.shape[0]
    lg = logits[:, :N_GROUPS]
    le = logits[:, N_GROUPS:N_GROUPS * (1 + EXPERTS_PER_GROUP)].reshape(t, N_GROUPS, EXPERTS_PER_GROUP)
    g_sel = jnp.argmax(lg + bg.astype(F32), axis=-1)
    p_group = jnp.take_along_axis(jax.nn.softmax(lg, axis=-1), g_sel[:, None], axis=1)
    le_sel = jnp.take_along_axis(le, g_sel[:, None, None], axis=1)[:, 0]
    _, top = lax.top_k(le_sel + be.astype(F32)[g_sel], TOP_K_IN_GROUP)
    p_exp = jax.nn.softmax(jnp.take_along_axis(le_sel, top, axis=1), axis=-1)
    gate = p_group * p_exp
    expert = (g_sel[:, None] * EXPERTS_PER_GROUP + top).astype(jnp.int32)
    return expert, gate


def moe_plan(expert, n_experts, n_hid_tiles):
    t, k = expert.shape
    a = t * k
    bm = MOE_BLOCK
    nt = n_hid_tiles
    i32 = jnp.int32
    flat_e = expert.reshape(-1)
    eids = jnp.arange(n_experts, dtype=i32)
    onehot = (flat_e[:, None] == eids[None, :]).astype(i32)
    csum = jnp.cumsum(onehot, axis=0)
    rank = jnp.sum(csum * onehot, axis=1) - 1
    counts = csum[-1]
    padded = (counts + bm - 1) // bm * bm
    ends_p = jnp.cumsum(padded)
    pstart = ends_p - padded
    dest = (jnp.sum(onehot * pstart[None, :], axis=1) + rank).astype(i32)
    n_blocks = (a + n_experts * (bm - 1) + bm - 1) // bm
    slot_tok = (jnp.arange(n_blocks * bm, dtype=i32) % t).at[dest].set((jnp.arange(a) // k).astype(i32))

    bidx = jnp.arange(n_blocks, dtype=i32)
    eoh = ((bidx[:, None] * bm >= pstart[None, :]) & (bidx[:, None] * bm < ends_p[None, :])).astype(i32)
    block_e = jnp.sum(eoh * eids[None, :], axis=1)
    used = (ends_p[-1] // bm).astype(i32)
    bi = bidx - jnp.sum(eoh * (pstart // bm)[None, :], axis=1)
    r = bi % MOE_CHUNK_BLOCKS
    cb = bidx - r
    nbc = jnp.minimum(MOE_CHUNK_BLOCKS, jnp.sum(eoh * (padded // bm)[None, :], axis=1) - (bi - r))
    valid_b = bidx < used
    ns = nt * n_blocks
    n_valid = nt * used
    sidx = jnp.arange(ns, dtype=i32)
    s_eff = jnp.minimum(sidx, jnp.maximum(n_valid - 1, 0))
    reach = (nt * cb[None, :] <= s_eff[:, None]) & valid_b[None, :]
    cb_s = jnp.max(jnp.where(reach, cb[None, :], 0), axis=1)
    at_cb = (bidx[None, :] == cb_s[:, None]).astype(i32)
    nbc_s = jnp.sum(at_cb * nbc[None, :], axis=1)
    e_s = jnp.sum(at_cb * block_e[None, :], axis=1)
    off = s_eff - nt * cb_s
    t_s = off // jnp.maximum(nbc_s, 1)
    r_s = off - t_s * nbc_s
    b_s = cb_s + r_s
    is_pad = sidx >= n_valid
    last = t_s == nt - 1
    pad_ob = jnp.minimum(used + (sidx - n_valid) // nt, n_blocks - 1)
    ob_s = jnp.where(is_pad, pad_ob, jnp.where(last, b_s, cb_s))
    nxt_cb = cb_s + nbc_s
    nxt_e_chunk = jnp.sum((bidx[None, :] == nxt_cb[:, None]).astype(i32) * block_e[None, :], axis=1)
    has_next = jnp.logical_not(last) | (nxt_cb < used)
    nse = jnp.where(last, nxt_e_chunk, e_s)
    nst = jnp.where(last, 0, t_s + 1)
    chunks_before = jnp.sum(((bidx[None, :] < cb_s[:, None]) & (r[None, :] == 0)).astype(i32), axis=1)
    run_parity = (chunks_before * nt + t_s) & 1
    flags = jnp.where(is_pad, 64,
                      1 + jnp.where(t_s == 0, 2, 0) + jnp.where(last, 4, 0) + jnp.where(r_s == 0, 8, 0)
                      + jnp.where(has_next, 16, 0)) + run_parity * 32
    table = jnp.stack([e_s, t_s, b_s, r_s, ob_s, flags, nse, nst]).astype(i32)
    return dest, slot_tok, table


def _moe_kernel(se, st, sxb, sr, sob, sfl, nse, nst, x_ref, w1_hbm, w3_hbm, w2_hbm, o_ref,
                wf1, wf3, wf2, w1b, w3b, w2b, acc, sem, *, layer, th):
    s = pl.program_id(0)
    fl = sfl[s]
    valid = (fl & 1) != 0
    first = (fl & 2) != 0
    last = (fl & 4) != 0
    load_w = (fl & 8) != 0
    has_next = (fl & 16) != 0
    slot = (fl >> 5) & 1

    @pl.when((fl & 64) != 0)
    def _():
        o_ref[...] = jnp.zeros(o_ref.shape, o_ref.dtype)

    def weight_copies(e, t, sl):
        c0 = pl.multiple_of(t * th, th)
        return (pltpu.make_async_copy(w1_hbm.at[layer, e, :, pl.ds(c0, th)], wf1.at[sl], sem.at[0, sl]),
                pltpu.make_async_copy(w3_hbm.at[layer, e, :, pl.ds(c0, th)], wf3.at[sl], sem.at[1, sl]),
                pltpu.make_async_copy(w2_hbm.at[layer, e, pl.ds(c0, th), :], wf2.at[sl], sem.at[2, sl]))

    @pl.when(valid & load_w)
    def _():
        @pl.when(s == 0)
        def _():
            for c in weight_copies(se[s], st[s], slot):
                c.start()

        for c in weight_copies(se[s], st[s], slot):
            c.wait()

        @pl.when(has_next)
        def _():
            for c in weight_copies(nse[s], nst[s], 1 - slot):
                c.start()

        w1b[...] = wf1[slot].astype(BF16)
        w3b[...] = wf3[slot].astype(BF16)
        w2b[...] = wf2[slot].astype(BF16)

    @pl.when(valid)
    def _():
        x = x_ref[...].astype(BF16)
        a = jnp.dot(x, w1b[...], preferred_element_type=F32)
        b = jnp.dot(x, w3b[...], preferred_element_type=F32)
        hdn = (_silu(a) * b).astype(BF16)
        y = jnp.dot(hdn, w2b[...], preferred_element_type=F32)
        r = sr[s]

        @pl.when(first & last)
        def _():
            o_ref[...] = y

        @pl.when(first & jnp.logical_not(last))
        def _():
            acc[r] = y

        @pl.when(jnp.logical_not(first) & jnp.logical_not(last))
        def _():
            acc[r] = acc[r] + y

        @pl.when(jnp.logical_not(first) & last)
        def _():
            o_ref[...] = acc[r] + y


def moe_experts(xs, table, w1, w3, w2, layer):
    np_, d = xs.shape
    de = w1.shape[-1]
    th = MOE_HID_TILE
    ns = table.shape[1]
    bm = MOE_BLOCK
    hbm = pl.BlockSpec(memory_space=pl.ANY)
    grid_spec = pltpu.PrefetchScalarGridSpec(
        num_scalar_prefetch=8,
        grid=(ns,),
        in_specs=[pl.BlockSpec((bm, d), lambda s, se, st, sxb, sr, sob, sfl, nse, nst: (sxb[s], 0)),
                  hbm, hbm, hbm],
        out_specs=pl.BlockSpec((bm, d), lambda s, se, st, sxb, sr, sob, sfl, nse, nst: (sob[s], 0)),
        scratch_shapes=[pltpu.VMEM((2, d, th), F32), pltpu.VMEM((2, d, th), F32), pltpu.VMEM((2, th, d), F32),
                        pltpu.VMEM((d, th), BF16), pltpu.VMEM((d, th), BF16), pltpu.VMEM((th, d), BF16),
                        pltpu.VMEM((MOE_CHUNK_BLOCKS, bm, d), F32),
                        pltpu.SemaphoreType.DMA((3, 2))],
    )
    return pl.pallas_call(
        functools.partial(_moe_kernel, layer=layer, th=th),
        grid_spec=grid_spec,
        out_shape=jax.ShapeDtypeStruct((np_, d), F32),
        compiler_params=_cparams(("arbitrary",)),
        name="moe_experts",
    )(*[table[i] for i in range(8)], xs, w1, w3, w2)


def _rows(a, reps):
    return jnp.repeat(a, reps, axis=0)


def kernel(x_prompt, x_sample, c_prompt, c_sample, cache_k, cache_v, state_hgrn, page_table, norm_mix, norm_ffn, ada_w, ada_b, hgrn_w_in, hgrn_lower_bound, hgrn_out_norm, hgrn_w_out, diff_w_in, diff_qk_norm, diff_lambda, diff_subln, diff_w_out, rel_bias_table, router_group_w, router_group_b, router_expert_w, router_expert_b, expert_w1, expert_w3, expert_w2):
    bp, lp, d = x_prompt.shape
    bs, ls, _ = x_sample.shape
    mp, ms = bp * lp, bs * ls
    depth = ada_w.shape[0]
    page = cache_k.shape[2]
    n_pages = page_table.shape[1]
    past_len = n_pages * page
    n_experts = expert_w1.shape[1]
    assert lp % ROW_TILE == 0 and ms % BF16_ROWS == 0
    assert MAX_DISTANCE <= min(ATT_TILE, lp) and lp >= 2 * min(ATT_TILE, lp)
    assert MAX_DISTANCE <= page and ls <= page
    assert expert_w1.shape[-1] % MOE_HID_TILE == 0
    assert n_pages % PAGES_PER_STEP == 0 and n_pages // PAGES_PER_STEP > 1

    xp = x_prompt.reshape(mp, d)
    xs = x_sample.reshape(ms, d)

    n_c = bp + bs
    pad = (-n_c) % BF16_ROWS
    c_all = jnp.concatenate([c_prompt, c_sample, jnp.zeros((pad, d), F32)], axis=0)
    mod = adaln_all(c_all, ada_w, ada_b)

    lb_prob = jax.nn.softmax(hgrn_lower_bound.astype(F32), axis=0)
    lower_bounds = jnp.cumsum(lb_prob, axis=0) - lb_prob[0]

    tile = min(ATT_TILE, lp)
    ar = jnp.arange(tile)
    near = [_toeplitz(_bias_by_distance(qoff + jnp.arange(3 * tile) - (2 * tile - 1), rel_bias_table), 2 * tile, tile)
            for qoff in (0, tile)]
    bias_near = jnp.stack([jnp.concatenate([b, b], axis=2) for b in near], axis=1)
    bias_far = rel_bias_table[_bucket(jnp.array(MAX_DISTANCE, jnp.int32))].astype(F32)
    pos_s = past_len + jnp.arange(ls)
    qrows = DF_HEADS * 2 * ls

    def dec_bias(b_hqk):
        hq = jnp.stack([b_hqk, b_hqk], axis=1).reshape(qrows, b_hqk.shape[-1])
        row_head = jnp.arange(qrows) // (2 * ls) % SUBLANES
        own = row_head[:, None] == jnp.arange(SUBLANES)[None, :]
        return jnp.where(own[:, None, :], hq[:, :, None], NEG_BIG).reshape(qrows, -1)

    far_hqk = jnp.broadcast_to(bias_far[:, None, None], (DF_HEADS, ls, page))
    dbias_far = dec_bias(far_hqk)
    dbias_last = dec_bias(_bias_tile(pos_s, past_len - page + jnp.arange(page), rel_bias_table))
    dbias_tail = jnp.stack([dbias_far, dbias_last])
    n_new = -(-ls // BF16_ROWS) * BF16_ROWS
    new_kpos = jnp.where(jnp.arange(n_new) < ls, past_len + jnp.arange(n_new), past_len + 2 * page)
    dbias_new = dec_bias(_bias_tile(pos_s, new_kpos, rel_bias_table))

    outs = {k: [] for k in ('k_p', 'v_p', 'k_s', 'v_s', 'st_p', 'st_s')}
    for i in range(depth):
        j = i // N_MIXERS
        mods = jnp.split(mod[i], 6, axis=-1)
        mp_ = [m[:bp].reshape(bp, 1, d) for m in mods]
        ms_ = [_rows(m[bp:bp + bs], ls) for m in mods]
        g_mix = norm_mix[i].reshape(1, d)
        g_ffn = norm_ffn[i].reshape(1, d)

        if i % N_MIXERS == 0:
            w_in = hgrn_w_in[j].astype(BF16)
            w_out = hgrn_w_out[j].astype(BF16)
            lb = lower_bounds[j].reshape(1, -1)
            onorm = hgrn_out_norm[j].astype(F32).reshape(1, -1)
            dk = onorm.shape[1]
            qfig_p = norm_mod_matmul(xp, g_mix, mp_[0], mp_[1], w_in, lp, ROW_TILE)
            qfig_s = norm_mod_matmul(xs, g_mix, ms_[0], ms_[1], w_in, ls, ms)
            s0_p = jnp.zeros((bp, HG_HEADS, dk, dk), F32)
            op, sp = hgrn_recurrence(qfig_p, lb, onorm, s0_p, bp, lp, BF16)
            os_, ss = hgrn_recurrence(qfig_s, lb, onorm, state_hgrn[j].astype(F32), bs, ls, F32)
            os_ = os_.astype(BF16)
            outs['st_p'].append(sp.astype(state_hgrn.dtype))
            outs['st_s'].append(ss.astype(state_hgrn.dtype))
        else:
            lam_init = 0.8 - 0.6 * math.exp(-0.3 * i)
            lpf = diff_lambda[j].astype(F32)
            lam = (jnp.exp(jnp.sum(lpf[0] * lpf[1])) - jnp.exp(jnp.sum(lpf[2] * lpf[3])) + lam_init).reshape(1)
            w_in = diff_w_in[j].astype(BF16)
            w_out = diff_w_out[j].astype(BF16)
            dh = diff_qk_norm.shape[-1]
            qn = jnp.tile(diff_qk_norm[j, 0], 2).reshape(1, 2 * dh) * (dh ** -0.5)
            kn = jnp.tile(diff_qk_norm[j, 1], 2).reshape(1, 2 * dh)
            subln = diff_subln[j].reshape(1, -1)
            q_p, k_p, v_p = norm_mod_qkv(xp, g_mix, mp_[0], mp_[1], w_in, qn, kn, lp, ROW_TILE)
            q_s, k_s, v_s = norm_mod_qkv(xs, g_mix, ms_[0], ms_[1], w_in, qn, kn, ls, ms)
            op = attn_prompt(q_p, k_p, v_p, lam, bias_far, bias_near, subln, bp, lp, 1.0 - lam_init)
            q4 = q_s.reshape(bs, ls, DF_HEADS, 2 * dh).transpose(0, 2, 1, 3)
            lo = jnp.arange(2 * dh) < dh
            qz = jnp.stack([jnp.where(lo, q4, 0), jnp.where(lo, 0, q4)], axis=2).reshape(bs, DF_HEADS * 2 * ls, 2 * dh)
            k_s5 = k_s.reshape(bs, ls, DF_HEADS, 2 * dh)
            v_s5 = v_s.reshape(bs, ls, DF_HEADS, 2 * dh)
            padw = ((0, 0), (0, n_new - ls), (0, 0), (0, 0))
            o_s = attn_decode(qz, cache_k, cache_v, j, jnp.pad(k_s5, padw), jnp.pad(v_s5, padw), page_table, lam,
                              dbias_far, dbias_tail, dbias_new, subln, 1.0 - lam_init)
            os_ = o_s.transpose(0, 2, 1, 3).reshape(ms, DF_HEADS * 2 * dh).astype(BF16)
            outs['k_p'].append(to_heads(k_p, DF_HEADS, COMBINE_TILE).reshape(bp, lp, DF_HEADS, 2 * dh))
            outs['v_p'].append(to_heads(v_p, DF_HEADS, COMBINE_TILE).reshape(bp, lp, DF_HEADS, 2 * dh))
            outs['k_s'].append(k_s5)
            outs['v_s'].append(v_s5)

        xp = matmul_residual(op, w_out, xp, mp_[2], lp, ROW_TILE)
        xs = matmul_residual(os_, w_out, xs, ms_[2], ls, ms)

        nr = LANES
        wr = jnp.concatenate([router_group_w[i], router_expert_w[i],
                              jnp.zeros((d, nr - N_GROUPS * (1 + EXPERTS_PER_GROUP)), F32)], axis=1)
        wr_hi = wr.astype(BF16)
        wr_lo = (wr - wr_hi.astype(F32)).astype(BF16)
        h_p, lg_p = norm_mod_router(xp, g_ffn, mp_[3], mp_[4], wr_hi, wr_lo, lp, ROW_TILE, mp + ms)
        h_s, lg_s = norm_mod_router(xs, g_ffn, ms_[3], ms_[4], wr_hi, wr_lo, ls, ms, ms)
        h_all = lax.dynamic_update_slice(h_p, h_s, (mp, 0))
        lg_all = lax.dynamic_update_slice(lg_p, lg_s, (mp, 0))
        expert, gate = route(lg_all, router_group_b[i], router_expert_b[i])
        n_hid_tiles = expert_w1.shape[-1] // MOE_HID_TILE
        dest, slot_tok, table = moe_plan(expert, n_experts, n_hid_tiles)
        yb = moe_experts(h_all[slot_tok], table, expert_w1, expert_w3, expert_w2, i)
        dest2 = dest.reshape(mp + ms, TOP_K_IN_GROUP)
        y0, y1 = yb[dest2[:, 0]], yb[dest2[:, 1]]
        g0, g1 = gate[:, 0:1], gate[:, 1:2]
        xp = moe_combine(xp, mp_[5], y0, y1, g0, g1, 0, lp, COMBINE_TILE)
        xs = moe_combine(xs, ms_[5], y0, y1, g0, g1, mp, ls, ms)

    return (xp.reshape(bp, lp, d), xs.reshape(bs, ls, d),
            jnp.stack(outs['k_p']), jnp.stack(outs['v_p']), jnp.stack(outs['k_s']), jnp.stack(outs['v_s']),
            jnp.stack(outs['st_p']), jnp.stack(outs['st_s']))
```

```python
import functools
import math

import jax
import jax.numpy as jnp
from jax import lax
from jax.experimental import pallas as pl
from jax.experimental.pallas import tpu as pltpu

F32 = jnp.float32
BF16 = jnp.bfloat16

N_MIXERS = 2
HG_HEADS = 16
DF_HEADS = 16
NUM_BUCKETS = 32
MAX_DISTANCE = 128
N_GROUPS = 4
EXPERTS_PER_GROUP = 8
TOP_K_IN_GROUP = 2
EPS = 1e-6
NEG_BIG = -1e30

LANES = 128
SUBLANES = 8
BF16_ROWS = 16
VMEM_LIMIT = 56 * 1024 * 1024

ROW_TILE = 1024
COL_TILE = 512
COMBINE_TILE = 512
HG_CHUNK = 64
HG_SUB = 16
HG_HEADS_PER_STEP = 4
ATT_HEADS_PER_STEP = 4
ATT_TILE = 256
PAGES_PER_STEP = 4
MOE_BLOCK = 256
MOE_HID_TILE = 512
MOE_CHUNK_BLOCKS = 4

NT_DIMS = (((1,), (1,)), ((), ()))
TN_DIMS = (((0,), (0,)), ((), ()))


def _cparams(sem):
    return pltpu.CompilerParams(dimension_semantics=sem, vmem_limit_bytes=VMEM_LIMIT)


def _silu(x):
    return x * jax.nn.sigmoid(x)


def _norm_mod(x, g, shift, scale):
    y = x * lax.rsqrt(jnp.mean(x * x, axis=-1, keepdims=True) + EPS)
    return (y * g) * (1.0 + scale) + shift


def _split3(x):
    x1 = x.astype(BF16)
    r1 = x - x1.astype(F32)
    x2 = r1.astype(BF16)
    x3 = (r1 - x2.astype(F32)).astype(BF16)
    return x1, x2, x3


def _adaln_kernel(c_ref, w_ref, b_ref, o_ref):
    a = _silu(c_ref[...]).astype(BF16)
    o_ref[...] = jnp.dot(a, w_ref[...].astype(BF16), preferred_element_type=F32) + b_ref[...]


def adaln_all(c_all, ada_w, ada_b, tn=1024):
    depth, d, n = ada_w.shape
    r = c_all.shape[0]
    return pl.pallas_call(
        _adaln_kernel,
        grid=(depth, n // tn),
        in_specs=[pl.BlockSpec((r, d), lambda l, j: (0, 0)),
                  pl.BlockSpec((None, d, tn), lambda l, j: (l, 0, j)),
                  pl.BlockSpec((None, 1, tn), lambda l, j: (l, 0, j))],
        out_specs=pl.BlockSpec((None, r, tn), lambda l, j: (l, 0, j)),
        out_shape=jax.ShapeDtypeStruct((depth, r, n), F32),
        compiler_params=_cparams(("arbitrary", "arbitrary")),
        name="adaln",
    )(c_all, ada_w, ada_b.reshape(depth, 1, n))


def _mod_spec(a, tm, rows_per_seq):
    if a.ndim == 3:
        blocks_per_seq = rows_per_seq // tm
        return pl.BlockSpec((None, 1, a.shape[-1]), lambda i, j: (i // blocks_per_seq, 0, 0))
    return pl.BlockSpec((tm, a.shape[-1]), lambda i, j: (i, 0))


def _mod_spec_cols(a, tm, tn, rows_per_seq):
    if a.ndim == 3:
        blocks_per_seq = rows_per_seq // tm
        return pl.BlockSpec((None, 1, tn), lambda i, j: (i // blocks_per_seq, 0, j))
    return pl.BlockSpec((tm, tn), lambda i, j: (i, j))


def _nmm_plain_kernel(x_ref, g_ref, sh_ref, sc_ref, w_ref, o_ref, h_scr):
    @pl.when(pl.program_id(1) == 0)
    def _():
        h_scr[...] = _norm_mod(x_ref[...], g_ref[...], sh_ref[...], sc_ref[...]).astype(BF16)

    o_ref[...] = jnp.dot(h_scr[...], w_ref[...], preferred_element_type=F32)


def _qk_norm_store(acc, wn, out_ref):
    tn = acc.shape[1]
    lo_mask = lax.broadcasted_iota(jnp.int32, (1, LANES), 1) < (LANES // 2)
    for gi in range(tn // LANES):
        s = acc[:, gi * LANES:(gi + 1) * LANES]
        sq = s * s
        lo = jnp.sum(jnp.where(lo_mask, sq, 0.0), axis=-1, keepdims=True)
        hi = jnp.sum(jnp.where(lo_mask, 0.0, sq), axis=-1, keepdims=True)
        ms = jnp.where(lo_mask, lo, hi) * (2.0 / LANES)
        out_ref[:, gi * LANES:(gi + 1) * LANES] = (s * lax.rsqrt(ms + EPS) * wn).astype(out_ref.dtype)


def _nmm_qkv_kernel(x_ref, g_ref, sh_ref, sc_ref, w_ref, qn_ref, kn_ref, q_ref, k_ref, v_ref, h_scr, *, nt):
    j = pl.program_id(1)

    @pl.when(j == 0)
    def _():
        h_scr[...] = _norm_mod(x_ref[...], g_ref[...], sh_ref[...], sc_ref[...]).astype(BF16)

    acc = jnp.dot(h_scr[...], w_ref[...], preferred_element_type=F32)

    @pl.when(j < nt)
    def _():
        _qk_norm_store(acc, qn_ref[...], q_ref)

    @pl.when((j >= nt) & (j < 2 * nt))
    def _():
        _qk_norm_store(acc, kn_ref[...], k_ref)

    @pl.when(j >= 2 * nt)
    def _():
        v_ref[...] = acc


def norm_mod_matmul(x, g, shift, scale, w, rows_per_seq, tm, tn=COL_TILE):
    m, d = x.shape
    n = w.shape[1]
    return pl.pallas_call(
        _nmm_plain_kernel,
        grid=(m // tm, n // tn),
        in_specs=[pl.BlockSpec((tm, d), lambda i, j: (i, 0)),
                  pl.BlockSpec((1, d), lambda i, j: (0, 0)),
                  _mod_spec(shift, tm, rows_per_seq),
                  _mod_spec(scale, tm, rows_per_seq),
                  pl.BlockSpec((d, tn), lambda i, j: (0, j))],
        out_specs=pl.BlockSpec((tm, tn), lambda i, j: (i, j)),
        out_shape=jax.ShapeDtypeStruct((m, n), F32),
        scratch_shapes=[pltpu.VMEM((tm, d), BF16)],
        compiler_params=_cparams(("arbitrary", "arbitrary")),
        name="norm_proj",
    )(x, g, shift, scale, w)


def norm_mod_qkv(x, g, shift, scale, w, qn, kn, rows_per_seq, tm, tn=COL_TILE):
    m, d = x.shape
    part = w.shape[1] // 3
    nt = part // tn
    clamp = lambda j, lo: jnp.clip(j - lo, 0, nt - 1)
    return pl.pallas_call(
        functools.partial(_nmm_qkv_kernel, nt=nt),
        grid=(m // tm, 3 * nt),
        in_specs=[pl.BlockSpec((tm, d), lambda i, j: (i, 0)),
                  pl.BlockSpec((1, d), lambda i, j: (0, 0)),
                  _mod_spec(shift, tm, rows_per_seq),
                  _mod_spec(scale, tm, rows_per_seq),
                  pl.BlockSpec((d, tn), lambda i, j: (0, j)),
                  pl.BlockSpec((1, LANES), lambda i, j: (0, 0)),
                  pl.BlockSpec((1, LANES), lambda i, j: (0, 0))],
        out_specs=[pl.BlockSpec((tm, tn), lambda i, j: (i, clamp(j, 0))),
                   pl.BlockSpec((tm, tn), lambda i, j: (i, clamp(j, nt))),
                   pl.BlockSpec((tm, tn), lambda i, j: (i, clamp(j, 2 * nt)))],
        out_shape=[jax.ShapeDtypeStruct((m, part), BF16),
                   jax.ShapeDtypeStruct((m, part), F32),
                   jax.ShapeDtypeStruct((m, part), F32)],
        scratch_shapes=[pltpu.VMEM((tm, d), BF16)],
        compiler_params=_cparams(("arbitrary", "arbitrary")),
        name="norm_qkv",
    )(x, g, shift, scale, w, qn, kn)


def _mmres_kernel(o_ref, w_ref, x_ref, ga_ref, out_ref):
    out_ref[...] = x_ref[...] + ga_ref[...] * jnp.dot(o_ref[...], w_ref[...], preferred_element_type=F32)


def matmul_residual(o, w, x, gate, rows_per_seq, tm, tn=COL_TILE):
    m, kdim = o.shape
    n = w.shape[1]
    return pl.pallas_call(
        _mmres_kernel,
        grid=(m // tm, n // tn),
        in_specs=[pl.BlockSpec((tm, kdim), lambda i, j: (i, 0)),
                  pl.BlockSpec((kdim, tn), lambda i, j: (0, j)),
                  pl.BlockSpec((tm, tn), lambda i, j: (i, j)),
                  _mod_spec_cols(gate, tm, tn, rows_per_seq)],
        out_specs=pl.BlockSpec((tm, tn), lambda i, j: (i, j)),
        out_shape=jax.ShapeDtypeStruct((m, n), F32),
        compiler_params=_cparams(("arbitrary", "arbitrary")),
        name="out_proj_residual",
    )(o, w, x, gate)


def _combine_kernel(x_ref, ga_ref, y0_ref, y1_ref, g0_ref, g1_ref, o_ref):
    o_ref[...] = x_ref[...] + ga_ref[...] * (g0_ref[...] * y0_ref[...] + g1_ref[...] * y1_ref[...])


def moe_combine(x, gate_mod, y0, y1, g0, g1, row0, rows_per_seq, tm):
    m, d = x.shape
    b0 = row0 // tm
    tok = lambda i, j: (b0 + i, 0)
    return pl.pallas_call(
        _combine_kernel,
        grid=(m // tm, 1),
        in_specs=[pl.BlockSpec((tm, d), lambda i, j: (i, 0)),
                  _mod_spec_cols(gate_mod, tm, d, rows_per_seq),
                  pl.BlockSpec((tm, d), tok), pl.BlockSpec((tm, d), tok),
                  pl.BlockSpec((tm, 1), tok), pl.BlockSpec((tm, 1), tok)],
        out_specs=pl.BlockSpec((tm, d), lambda i, j: (i, 0)),
        out_shape=jax.ShapeDtypeStruct((m, d), F32),
        compiler_params=_cparams(("arbitrary", "arbitrary")),
        name="moe_combine",
    )(x, gate_mod, y0, y1, g0, g1)


def _to_heads_kernel(x_ref, o_ref):
    for h in range(o_ref.shape[1]):
        o_ref[:, h, :] = x_ref[:, h * LANES:(h + 1) * LANES]


def to_heads(x, heads, tm):
    m, width = x.shape
    return pl.pallas_call(
        _to_heads_kernel,
        grid=(m // tm,),
        in_specs=[pl.BlockSpec((tm, width), lambda i: (i, 0))],
        out_specs=pl.BlockSpec((tm, heads, LANES), lambda i: (i, 0, 0)),
        out_shape=jax.ShapeDtypeStruct((m, heads, LANES), x.dtype),
        compiler_params=_cparams(("arbitrary",)),
        name="to_heads",
    )(x)


def _nrouter_kernel(x_ref, g_ref, sh_ref, sc_ref, wh_ref, wl_ref, h_ref, lg_ref, *, n_in_blocks):
    i = pl.program_id(0)

    @pl.when(i < n_in_blocks)
    def _():
        h = _norm_mod(x_ref[...], g_ref[...], sh_ref[...], sc_ref[...])
        h_ref[...] = h
        h1 = h.astype(BF16)
        h2 = (h - h1.astype(F32)).astype(BF16)
        wh = wh_ref[...]
        lg_ref[...] = (jnp.dot(h1, wh, preferred_element_type=F32)
                       + jnp.dot(h1, wl_ref[...], preferred_element_type=F32)
                       + jnp.dot(h2, wh, preferred_element_type=F32))

    @pl.when(i >= n_in_blocks)
    def _():
        h_ref[...] = jnp.zeros(h_ref.shape, h_ref.dtype)
        lg_ref[...] = jnp.zeros(lg_ref.shape, lg_ref.dtype)


def norm_mod_router(x, g, shift, scale, wr_hi, wr_lo, rows_per_seq, tm, out_rows):
    m, d = x.shape
    nr = wr_hi.shape[1]
    nb = m // tm
    clamp = lambda f: (lambda i: f(jnp.minimum(i, nb - 1), 0))
    sh_spec = _mod_spec(shift, tm, rows_per_seq)
    sc_spec = _mod_spec(scale, tm, rows_per_seq)
    return pl.pallas_call(
        functools.partial(_nrouter_kernel, n_in_blocks=nb),
        grid=(pl.cdiv(out_rows, tm),),
        in_specs=[pl.BlockSpec((tm, d), clamp(lambda i, j: (i, 0))),
                  pl.BlockSpec((1, d), lambda i: (0, 0)),
                  pl.BlockSpec(sh_spec.block_shape, clamp(sh_spec.index_map)),
                  pl.BlockSpec(sc_spec.block_shape, clamp(sc_spec.index_map)),
                  pl.BlockSpec((d, nr), lambda i: (0, 0)),
                  pl.BlockSpec((d, nr), lambda i: (0, 0))],
        out_specs=[pl.BlockSpec((tm, d), lambda i: (i, 0)),
                   pl.BlockSpec((tm, nr), lambda i: (i, 0))],
        out_shape=[jax.ShapeDtypeStruct((out_rows, d), F32),
                   jax.ShapeDtypeStruct((out_rows, nr), F32)],
        compiler_params=_cparams(("arbitrary",)),
        name="norm_router",
    )(x, g, shift, scale, wr_hi, wr_lo)


def _chunk_cumsum(x, tri):
    c = x.shape[0]
    if tri is None:
        row = lax.broadcasted_iota(jnp.int32, x.shape, 0)
        out = jnp.zeros_like(x)
        for s in range(c):
            out = out + jnp.where(row >= s, x[s:s + 1], 0.0)
        return out
    x1, x2, x3 = _split3(x)
    return (jnp.dot(tri, x1, preferred_element_type=F32)
            + jnp.dot(tri, x2, preferred_element_type=F32)
            + jnp.dot(tri, x3, preferred_element_type=F32))


def _hgrn_kernel(q_ref, f_ref, i_ref, g_ref, lb_ref, on_ref, s0_ref, o_ref, sn_ref, st_scr, *, seq, chunk, sub, hb):
    lb = lb_ref[...]
    on = on_ref[...]
    for hh in range(hb):
        st_scr[hh] = s0_ref[hh].T
    nsub = chunk // sub
    row_sub = lax.broadcasted_iota(jnp.int32, (sub, LANES), 0)
    if chunk > BF16_ROWS:
        tri = (lax.broadcasted_iota(jnp.int32, (chunk, chunk), 0)
               >= lax.broadcasted_iota(jnp.int32, (chunk, chunk), 1)).astype(BF16)
    else:
        tri = None

    def chunk_body(ci, carry):
        r0 = pl.multiple_of(ci * chunk, chunk)
        q = q_ref[pl.ds(r0, chunk), :]
        f = f_ref[pl.ds(r0, chunk), :]
        v = i_ref[pl.ds(r0, chunk), :]
        g = g_ref[pl.ds(r0, chunk), :]
        qs_all = _silu(q)
        log_f = jnp.log(lb + (1.0 - lb) * jax.nn.sigmoid(f))
        k_all = (1.0 - lb) * jax.nn.sigmoid(-f)
        b_all = _chunk_cumsum(log_f, tri)
        sg_all = _silu(g)
        outs = []
        for hh in range(hb):
            cs = slice(hh * LANES, (hh + 1) * LANES)
            qs, k, b, vh = qs_all[:, cs], k_all[:, cs], b_all[:, cs], v[:, cs]
            st = st_scr[hh]
            vb = vh.astype(BF16)
            o_inter = lax.dot_general((qs * jnp.exp(b)).astype(BF16), st.astype(BF16), NT_DIMS,
                                      preferred_element_type=F32)
            parts = []
            for blk in range(nsub):
                lo = blk * sub
                q_b, b_b, k_b, v_b = qs[lo:lo + sub], b[lo:lo + sub], k[lo:lo + sub], vh[lo:lo + sub]
                o_b = o_inter[lo:lo + sub]
                if blk > 0:
                    b_ref_row = b[lo - 1:lo]
                    qd = (q_b * jnp.exp(b_b - b_ref_row)).astype(BF16)
                    kd = (k[:lo] * jnp.exp(b_ref_row - b[:lo])).astype(BF16)
                    sc = lax.dot_general(qd, kd, NT_DIMS, preferred_element_type=F32)
                    o_b = o_b + jnp.dot(sc.astype(BF16), vb[:lo], preferred_element_type=F32)
                for s in range(sub):
                    e = jnp.exp(b_b - b_b[s:s + 1])
                    a = jnp.where(row_sub >= s, q_b * e * k_b[s:s + 1], 0.0)
                    o_b = o_b + jnp.sum(a, axis=-1, keepdims=True) * v_b[s:s + 1]
                parts.append(o_b)
            o = parts[0] if nsub == 1 else jnp.concatenate(parts, axis=0)
            b_last = b[chunk - 1:chunk]
            kdc = k * jnp.exp(b_last - b)
            k1 = kdc.astype(BF16)
            k2 = (kdc - k1.astype(F32)).astype(BF16)
            v2 = (vh - vb.astype(F32)).astype(BF16)
            upd = (lax.dot_general(vb, k1, TN_DIMS, preferred_element_type=F32)
                   + lax.dot_general(vb, k2, TN_DIMS, preferred_element_type=F32)
                   + lax.dot_general(v2, k1, TN_DIMS, preferred_element_type=F32))
            st_scr[hh] = st * jnp.exp(b_last) + upd
            outs.append(o * lax.rsqrt(jnp.mean(o * o, axis=-1, keepdims=True) + EPS) * on * sg_all[:, cs])
        y = outs[0] if hb == 1 else jnp.concatenate(outs, axis=1)
        o_ref[pl.ds(r0, chunk), :] = y.astype(o_ref.dtype)
        return carry

    lax.fori_loop(0, seq // chunk, chunk_body, 0)
    for hh in range(hb):
        sn_ref[hh] = st_scr[hh].T


def hgrn_recurrence(qfig, lb, out_norm, s0, batch, seq, out_dtype):
    heads = s0.shape[1]
    dk = s0.shape[2]
    hb = HG_HEADS_PER_STEP
    ng = heads // hb
    chunk = min(HG_CHUNK, seq)
    sub = min(HG_SUB, chunk)
    col = lambda part: (lambda b, h: (b, part * ng + h))
    return pl.pallas_call(
        functools.partial(_hgrn_kernel, seq=seq, chunk=chunk, sub=sub, hb=hb),
        grid=(batch, ng),
        in_specs=[pl.BlockSpec((seq, hb * dk), col(0)),
                  pl.BlockSpec((seq, hb * dk), col(1)),
                  pl.BlockSpec((seq, hb * dk), col(2)),
                  pl.BlockSpec((seq, hb * dk), col(3)),
                  pl.BlockSpec((1, hb * dk), lambda b, h: (0, h)),
                  pl.BlockSpec((1, dk), lambda b, h: (0, 0)),
                  pl.BlockSpec((None, hb, dk, dk), lambda b, h: (b, h, 0, 0))],
        out_specs=[pl.BlockSpec((seq, hb * dk), lambda b, h: (b, h)),
                   pl.BlockSpec((None, hb, dk, dk), lambda b, h: (b, h, 0, 0))],
        out_shape=[jax.ShapeDtypeStruct((batch * seq, heads * dk), out_dtype),
                   jax.ShapeDtypeStruct(s0.shape, F32)],
        scratch_shapes=[pltpu.VMEM((hb, dk, dk), F32)],
        compiler_params=_cparams(("arbitrary", "arbitrary")),
        name="hgrn_recurrence",
    )(qfig, qfig, qfig, qfig, lb, out_norm, s0)


def _bucket(n):
    max_exact = NUM_BUCKETS // 2
    large = max_exact + (jnp.log(jnp.maximum(n, max_exact).astype(F32) / max_exact)
                         / math.log(MAX_DISTANCE / max_exact) * (NUM_BUCKETS - max_exact)).astype(jnp.int32)
    return jnp.where(n < max_exact, n, jnp.minimum(large, NUM_BUCKETS - 1))


def _bias_tile(qpos, kpos, table):
    n = jnp.maximum(qpos[:, None] - kpos[None, :], 0)
    bias = jnp.moveaxis(table[_bucket(n)], -1, 0).astype(F32)
    return jnp.where((qpos[:, None] >= kpos[None, :])[None], bias, NEG_BIG)


def _bias_by_distance(n, table):
    b = table[_bucket(jnp.maximum(n, 0))].astype(F32).T
    return jnp.where(n[None, :] >= 0, b, NEG_BIG)


def _toeplitz(gvec, nrow, ncol):
    h, p = gvec.shape
    skew = jnp.tile(gvec, (1, nrow))[:, :nrow * (p - 1)].reshape(h, nrow, p - 1)
    return skew[:, :, nrow - 1:nrow - 1 + ncol]


def _attn_prompt_kernel(lam_ref, bc_ref, q_ref, k_ref, v_ref, bn_ref, sub_ref, o_ref, kb, vb,
                        *, tile, hb, out_scale):
    hg = pl.program_id(1)
    qi = pl.program_id(2)

    cols = [slice(hh * LANES, (hh + 1) * LANES) for hh in range(hb)]

    @pl.when(qi == 0)
    def _():
        kb[...] = k_ref[...].astype(BF16)
        for hh in range(hb):
            vb[hh] = v_ref[:, cols[hh]].T.astype(BF16)

    q = q_ref[...].astype(F32)
    lo_mask = lax.broadcasted_iota(jnp.int32, (LANES, tile), 0) < (LANES // 2)
    qqs = []
    for cs in cols:
        qt = q[:, cs].T
        qqs.append(jnp.concatenate([jnp.where(lo_mask, qt, 0.0), jnp.where(lo_mask, 0.0, qt)], axis=1).astype(BF16))

    nch = 2 * tile // LANES

    def step(qq, kt, vt, bias, carry):
        out = []
        for c in range(nch):
            m, l, acc = carry[c]
            b = bias(c) if callable(bias) else bias
            s = jnp.dot(kt, qq[:, c * LANES:(c + 1) * LANES], preferred_element_type=F32) + b
            m_new = jnp.maximum(m, jnp.max(s, axis=0, keepdims=True))
            alpha = jnp.exp(m - m_new)
            p = jnp.exp(s - m_new)
            l = l * alpha + jnp.sum(p, axis=0, keepdims=True)
            acc = acc * alpha + jnp.dot(vt, p.astype(BF16), preferred_element_type=F32)
            out.append((m_new, l, acc))
        return tuple(out)

    def far_body(kv, carry):
        r0 = pl.multiple_of(kv * tile, tile)
        return tuple(step(qqs[hh], kb[pl.ds(r0, tile), cols[hh]], vb[hh, :, pl.ds(r0, tile)],
                          bc_ref[hg * hb + hh], carry[hh]) for hh in range(hb))

    init = tuple(tuple((jnp.full((1, LANES), NEG_BIG, F32), jnp.zeros((1, LANES), F32),
                        jnp.zeros((LANES, LANES), F32)) for _ in range(nch)) for _ in range(hb))
    n_far = jnp.maximum(qi - 1, 0)
    carry = lax.fori_loop(0, n_far, far_body, init)
    n0 = pl.multiple_of(n_far * tile, tile)
    outs = []
    for hh in range(hb):
        fin = step(qqs[hh], kb[pl.ds(n0, 2 * tile), cols[hh]], vb[hh, :, pl.ds(n0, 2 * tile)],
                   lambda c, hh=hh: bn_ref[hh, :, c * LANES:(c + 1) * LANES], carry[hh])
        o = jnp.concatenate([acc / l for _, l, acc in fin], axis=1)
        od = o[:, :tile] - lam_ref[0] * o[:, tile:]
        yt = od * lax.rsqrt(jnp.mean(od * od, axis=0, keepdims=True) + EPS)
        outs.append(yt.T * sub_ref[...] * out_scale)
    y = outs[0] if hb == 1 else jnp.concatenate(outs, axis=1)
    o_ref[...] = y.astype(o_ref.dtype)


def attn_prompt(q, k, v, lam, bias_far, bias_near, subln, batch, seq, out_scale):
    heads = q.shape[1] // LANES
    hb = ATT_HEADS_PER_STEP
    tile = min(ATT_TILE, seq)
    nq = seq // tile
    smem = pl.BlockSpec(memory_space=pltpu.SMEM)
    return pl.pallas_call(
        functools.partial(_attn_prompt_kernel, tile=tile, hb=hb, out_scale=out_scale),
        grid=(batch, heads // hb, nq),
        in_specs=[smem, smem,
                  pl.BlockSpec((tile, hb * LANES), lambda b, h, i: (b * nq + i, h)),
                  pl.BlockSpec((seq, hb * LANES), lambda b, h, i: (b, h)),
                  pl.BlockSpec((seq, hb * LANES), lambda b, h, i: (b, h)),
                  pl.BlockSpec((hb, None, 2 * tile, 2 * tile), lambda b, h, i: (h, jnp.minimum(i, 1), 0, 0)),
                  pl.BlockSpec((1, LANES), lambda b, h, i: (0, 0))],
        out_specs=pl.BlockSpec((tile, hb * LANES), lambda b, h, i: (b * nq + i, h)),
        out_shape=jax.ShapeDtypeStruct(q.shape, BF16),
        scratch_shapes=[pltpu.VMEM((seq, hb * LANES), BF16), pltpu.VMEM((hb, LANES, seq), BF16)],
        compiler_params=_cparams(("arbitrary", "arbitrary", "arbitrary")),
        name="attn_prompt",
    )(lam, bias_far, q, k, v, bias_near, subln)


def _attn_decode_kernel(pt_ref, lam_ref, q_ref, *refs, npg, heads, out_scale):
    k_refs = refs[:npg]
    v_refs = refs[npg:2 * npg]
    kn_ref, vn_ref, bf_ref, bt_ref, bn_ref, sub_ref, o_ref, m_scr, l_scr, acc_scr = refs[2 * npg:]
    p = pl.program_id(1)
    ngrp = heads // SUBLANES
    grows = q_ref.shape[0] // ngrp
    dh = q_ref.shape[1]

    @pl.when(p == 0)
    def _():
        m_scr[...] = jnp.full(m_scr.shape, NEG_BIG, F32)
        l_scr[...] = jnp.zeros(l_scr.shape, F32)
        acc_scr[...] = jnp.zeros(acc_scr.shape, F32)

    q = q_ref[...]

    def pages(k_list, v_list, b_list):
        for a in range(ngrp):
            rs = slice(a * grows, (a + 1) * grows)
            hs = slice(a * SUBLANES, (a + 1) * SUBLANES)
            ss = [lax.dot_general(q[rs], k_ref[:, hs, :].reshape(-1, dh).astype(BF16), NT_DIMS,
                                  preferred_element_type=F32) + b_ref[rs, :]
                  for k_ref, b_ref in zip(k_list, b_list)]
            m_old = m_scr[rs, :]
            m_new = m_old
            for s in ss:
                m_new = jnp.maximum(m_new, jnp.max(s, axis=-1, keepdims=True))
            alpha = jnp.exp(m_old - m_new)
            l_new = l_scr[rs, :] * alpha
            acc_new = acc_scr[rs, :] * alpha
            for s, v_ref in zip(ss, v_list):
                pe = jnp.exp(s - m_new)
                l_new = l_new + jnp.sum(pe, axis=-1, keepdims=True)
                acc_new = acc_new + jnp.dot(pe.astype(BF16), v_ref[:, hs, :].reshape(-1, dh).astype(BF16),
                                            preferred_element_type=F32)
            l_scr[rs, :] = l_new
            acc_scr[rs, :] = acc_new
            m_scr[rs, :] = m_new

    pages(k_refs, v_refs, [bf_ref] * (npg - 1) + [bt_ref])

    @pl.when(p == pl.num_programs(1) - 1)
    def _():
        pages([kn_ref], [vn_ref], [bn_ref])
        o = acc_scr[...] / l_scr[...]
        rows = q_ref.shape[0] // heads
        o4 = o.reshape(heads, 2, rows // 2, LANES)
        od = o4[:, 0] - lam_ref[0] * o4[:, 1]
        y = od * lax.rsqrt(jnp.mean(od * od, axis=-1, keepdims=True) + EPS) * sub_ref[...] * out_scale
        o_ref[...] = y


def attn_decode(qz, cache_k, cache_v, layer, k_new, v_new, page_table, lam, bias_far, bias_tail, bias_new,
                subln, out_scale):
    bd, qrows, _ = qz.shape
    _, _, page, heads, dh = cache_k.shape
    n_pages = page_table.shape[1]
    npg = PAGES_PER_STEP
    nsteps = n_pages // npg
    nq = qrows // heads // 2
    bcols = page * SUBLANES
    page_spec = lambda gi: pl.BlockSpec(
        (None, None, page, heads, dh), lambda b, p, pt: (layer, pt[b, p * npg + gi], 0, 0, 0))
    const2 = lambda b, p, pt: (0, 0)
    new_spec = pl.BlockSpec((None, k_new.shape[1], heads, dh), lambda b, p, pt: (b, 0, 0, 0))
    grid_spec = pltpu.PrefetchScalarGridSpec(
        num_scalar_prefetch=1,
        grid=(bd, nsteps),
        in_specs=[pl.BlockSpec(memory_space=pltpu.SMEM),
                  pl.BlockSpec((None, qrows, dh), lambda b, p, pt: (b, 0, 0))]
                 + [page_spec(gi) for gi in range(npg)] + [page_spec(gi) for gi in range(npg)]
                 + [new_spec, new_spec,
                    pl.BlockSpec((qrows, bcols), const2),
                    pl.BlockSpec((None, qrows, bcols), lambda b, p, pt: (p // (nsteps - 1), 0, 0)),
                    pl.BlockSpec((qrows, bias_new.shape[1]), const2), pl.BlockSpec((1, dh), const2)],
        out_specs=pl.BlockSpec((None, heads, nq, dh), lambda b, p, pt: (b, 0, 0, 0)),
        scratch_shapes=[pltpu.VMEM((qrows, 1), F32), pltpu.VMEM((qrows, 1), F32), pltpu.VMEM((qrows, dh), F32)],
    )
    return pl.pallas_call(
        functools.partial(_attn_decode_kernel, npg=npg, heads=heads, out_scale=out_scale),
        grid_spec=grid_spec,
        out_shape=jax.ShapeDtypeStruct((bd, heads, nq, dh), F32),
        compiler_params=_cparams(("arbitrary", "arbitrary")),
        name="attn_decode",
    )(page_table, lam, qz, *([cache_k] * npg), *([cache_v] * npg), k_new, v_new,
      bias_far, bias_tail, bias_new, subln)


def route(logits, bg, be):
    t = logits.shape[0]
    lg = logits[:, :N_GROUPS]
    le = logits[:, N_GROUPS:N_GROUPS * (1 + EXPERTS_PER_GROUP)].reshape(t, N_GROUPS, EXPERTS_PER_GROUP)
    g_sel = jnp.argmax(lg + bg.astype(F32), axis=-1)
    p_group = jnp.take_along_axis(jax.nn.softmax(lg, axis=-1), g_sel[:, None], axis=1)
    le_sel = jnp.take_along_axis(le, g_sel[:, None, None], axis=1)[:, 0]
    _, top = lax.top_k(le_sel + be.astype(F32)[g_sel], TOP_K_IN_GROUP)
    p_exp = jax.nn.softmax(jnp.take_along_axis(le_sel, top, axis=1), axis=-1)
    gate = p_group * p_exp
    expert = (g_sel[:, None] * EXPERTS_PER_GROUP + top).astype(jnp.int32)
    return expert, gate


def moe_plan(expert, n_experts, n_hid_tiles):
    t, k = expert.shape
    a = t * k
    bm = MOE_BLOCK
    nt = n_hid_tiles
    i32 = jnp.int32
    flat_e = expert.reshape(-1)
    eids = jnp.arange(n_experts, dtype=i32)
    onehot = (flat_e[:, None] == eids[None, :]).astype(i32)
    csum = jnp.cumsum(onehot, axis=0)
    rank = jnp.sum(csum * onehot, axis=1) - 1
    counts = csum[-1]
    padded = (counts + bm - 1) // bm * bm
    ends_p = jnp.cumsum(padded)
    pstart = ends_p - padded
    dest = (jnp.sum(onehot * pstart[None, :], axis=1) + rank).astype(i32)
    n_blocks = (a + n_experts * (bm - 1) + bm - 1) // bm
    slot_tok = (jnp.arange(n_blocks * bm, dtype=i32) % t).at[dest].set((jnp.arange(a) // k).astype(i32))

    bidx = jnp.arange(n_blocks, dtype=i32)
    eoh = ((bidx[:, None] * bm >= pstart[None, :]) & (bidx[:, None] * bm < ends_p[None, :])).astype(i32)
    block_e = jnp.sum(eoh * eids[None, :], axis=1)
    used = (ends_p[-1] // bm).astype(i32)
    bi = bidx - jnp.sum(eoh * (pstart // bm)[None, :], axis=1)
    r = bi % MOE_CHUNK_BLOCKS
    cb = bidx - r
    nbc = jnp.minimum(MOE_CHUNK_BLOCKS, jnp.sum(eoh * (padded // bm)[None, :], axis=1) - (bi - r))
    valid_b = bidx < used
    ns = nt * n_blocks
    n_valid = nt * used
    sidx = jnp.arange(ns, dtype=i32)
    s_eff = jnp.minimum(sidx, jnp.maximum(n_valid - 1, 0))
    reach = (nt * cb[None, :] <= s_eff[:, None]) & valid_b[None, :]
    cb_s = jnp.max(jnp.where(reach, cb[None, :], 0), axis=1)
    at_cb = (bidx[None, :] == cb_s[:, None]).astype(i32)
    nbc_s = jnp.sum(at_cb * nbc[None, :], axis=1)
    e_s = jnp.sum(at_cb * block_e[None, :], axis=1)
    off = s_eff - nt * cb_s
    t_s = off // jnp.maximum(nbc_s, 1)
    r_s = off - t_s * nbc_s
    b_s = cb_s + r_s
    is_pad = sidx >= n_valid
    last = t_s == nt - 1
    pad_ob = jnp.minimum(used + (sidx - n_valid) // nt, n_blocks - 1)
    ob_s = jnp.where(is_pad, pad_ob, jnp.where(last, b_s, cb_s))
    nxt_cb = cb_s + nbc_s
    nxt_e_chunk = jnp.sum((bidx[None, :] == nxt_cb[:, None]).astype(i32) * block_e[None, :], axis=1)
    has_next = jnp.logical_not(last) | (nxt_cb < used)
    nse = jnp.where(last, nxt_e_chunk, e_s)
    nst = jnp.where(last, 0, t_s + 1)
    chunks_before = jnp.sum(((bidx[None, :] < cb_s[:, None]) & (r[None, :] == 0)).astype(i32), axis=1)
    run_parity = (chunks_before * nt + t_s) & 1
    flags = jnp.where(is_pad, 64,
                      1 + jnp.where(t_s == 0, 2, 0) + jnp.where(last, 4, 0) + jnp.where(r_s == 0, 8, 0)
                      + jnp.where(has_next, 16, 0)) + run_parity * 32
    table = jnp.stack([e_s, t_s, b_s, r_s, ob_s, flags, nse, nst]).astype(i32)
    return dest, slot_tok, table


def _moe_kernel(se, st, sxb, sr, sob, sfl, nse, nst, x_ref, w1_hbm, w3_hbm, w2_hbm, o_ref,
                wf1, wf3, wf2, w1b, w3b, w2b, acc, sem, *, layer, th):
    s = pl.program_id(0)
    fl = sfl[s]
    valid = (fl & 1) != 0
    first = (fl & 2) != 0
    last = (fl & 4) != 0
    load_w = (fl & 8) != 0
    has_next = (fl & 16) != 0
    slot = (fl >> 5) & 1

    @pl.when((fl & 64) != 0)
    def _():
        o_ref[...] = jnp.zeros(o_ref.shape, o_ref.dtype)

    def weight_copies(e, t, sl):
        c0 = pl.multiple_of(t * th, th)
        return (pltpu.make_async_copy(w1_hbm.at[layer, e, :, pl.ds(c0, th)], wf1.at[sl], sem.at[0, sl]),
                pltpu.make_async_copy(w3_hbm.at[layer, e, :, pl.ds(c0, th)], wf3.at[sl], sem.at[1, sl]),
                pltpu.make_async_copy(w2_hbm.at[layer, e, pl.ds(c0, th), :], wf2.at[sl], sem.at[2, sl]))

    @pl.when(valid & load_w)
    def _():
        @pl.when(s == 0)
        def _():
            for c in weight_copies(se[s], st[s], slot):
                c.start()

        for c in weight_copies(se[s], st[s], slot):
            c.wait()

        @pl.when(has_next)
        def _():
            for c in weight_copies(nse[s], nst[s], 1 - slot):
                c.start()

        w1b[...] = wf1[slot].astype(BF16)
        w3b[...] = wf3[slot].astype(BF16)
        w2b[...] = wf2[slot].astype(BF16)

    @pl.when(valid)
    def _():
        x = x_ref[...].astype(BF16)
        a = jnp.dot(x, w1b[...], preferred_element_type=F32)
        b = jnp.dot(x, w3b[...], preferred_element_type=F32)
        hdn = (_silu(a) * b).astype(BF16)
        y = jnp.dot(hdn, w2b[...], preferred_element_type=F32)
        r = sr[s]

        @pl.when(first & last)
        def _():
            o_ref[...] = y

        @pl.when(first & jnp.logical_not(last))
        def _():
            acc[r] = y

        @pl.when(jnp.logical_not(first) & jnp.logical_not(last))
        def _():
            acc[r] = acc[r] + y

        @pl.when(jnp.logical_not(first) & last)
        def _():
            o_ref[...] = acc[r] + y


def moe_experts(xs, table, w1, w3, w2, layer):
    np_, d = xs.shape
    de = w1.shape[-1]
    th = MOE_HID_TILE
    ns = table.shape[1]
    bm = MOE_BLOCK
    hbm = pl.BlockSpec(memory_space=pl.ANY)
    grid_spec = pltpu.PrefetchScalarGridSpec(
        num_scalar_prefetch=8,
        grid=(ns,),
        in_specs=[pl.BlockSpec((bm, d), lambda s, se, st, sxb, sr, sob, sfl, nse, nst: (sxb[s], 0)),
                  hbm, hbm, hbm],
        out_specs=pl.BlockSpec((bm, d), lambda s, se, st, sxb, sr, sob, sfl, nse, nst: (sob[s], 0)),
        scratch_shapes=[pltpu.VMEM((2, d, th), F32), pltpu.VMEM((2, d, th), F32), pltpu.VMEM((2, th, d), F32),
                        pltpu.VMEM((d, th), BF16), pltpu.VMEM((d, th), BF16), pltpu.VMEM((th, d), BF16),
                        pltpu.VMEM((MOE_CHUNK_BLOCKS, bm, d), F32),
                        pltpu.SemaphoreType.DMA((3, 2))],
    )
    return pl.pallas_call(
        functools.partial(_moe_kernel, layer=layer, th=th),
        grid_spec=grid_spec,
        out_shape=jax.ShapeDtypeStruct((np_, d), F32),
        compiler_params=_cparams(("arbitrary",)),
        name="moe_experts",
    )(*[table[i] for i in range(8)], xs, w1, w3, w2)


def _rows(a, reps):
    return jnp.repeat(a, reps, axis=0)


def kernel(x_prompt, x_sample, c_prompt, c_sample, cache_k, cache_v, state_hgrn, page_table, norm_mix, norm_ffn, ada_w, ada_b, hgrn_w_in, hgrn_lower_bound, hgrn_out_norm, hgrn_w_out, diff_w_in, diff_qk_norm, diff_lambda, diff_subln, diff_w_out, rel_bias_table, router_group_w, router_group_b, router_expert_w, router_expert_b, expert_w1, expert_w3, expert_w2):
    bp, lp, d = x_prompt.shape
    bs, ls, _ = x_sample.shape
    mp, ms = bp * lp, bs * ls
    depth = ada_w.shape[0]
    page = cache_k.shape[2]
    n_pages = page_table.shape[1]
    past_len = n_pages * page
    n_experts = expert_w1.shape[1]
    assert lp % ROW_TILE == 0 and ms % BF16_ROWS == 0
    assert MAX_DISTANCE <= min(ATT_TILE, lp) and lp >= 2 * min(ATT_TILE, lp)
    assert MAX_DISTANCE <= page and ls <= page
    assert expert_w1.shape[-1] % MOE_HID_TILE == 0
    assert n_pages % PAGES_PER_STEP == 0 and n_pages // PAGES_PER_STEP > 1

    xp = x_prompt.reshape(mp, d)
    xs = x_sample.reshape(ms, d)

    n_c = bp + bs
    pad = (-n_c) % BF16_ROWS
    c_all = jnp.concatenate([c_prompt, c_sample, jnp.zeros((pad, d), F32)], axis=0)
    mod = adaln_all(c_all, ada_w, ada_b)

    lb_prob = jax.nn.softmax(hgrn_lower_bound.astype(F32), axis=0)
    lower_bounds = jnp.cumsum(lb_prob, axis=0) - lb_prob[0]

    tile = min(ATT_TILE, lp)
    ar = jnp.arange(tile)
    near = [_toeplitz(_bias_by_distance(qoff + jnp.arange(3 * tile) - (2 * tile - 1), rel_bias_table), 2 * tile, tile)
            for qoff in (0, tile)]
    bias_near = jnp.stack([jnp.concatenate([b, b], axis=2) for b in near], axis=1)
    bias_far = rel_bias_table[_bucket(jnp.array(MAX_DISTANCE, jnp.int32))].astype(F32)
    pos_s = past_len + jnp.arange(ls)
    qrows = DF_HEADS * 2 * ls

    def dec_bias(b_hqk):
        hq = jnp.stack([b_hqk, b_hqk], axis=1).reshape(qrows, b_hqk.shape[-1])
        row_head = jnp.arange(qrows) // (2 * ls) % SUBLANES
        own = row_head[:, None] == jnp.arange(SUBLANES)[None, :]
        return jnp.where(own[:, None, :], hq[:, :, None], NEG_BIG).reshape(qrows, -1)

    far_hqk = jnp.broadcast_to(bias_far[:, None, None], (DF_HEADS, ls, page))
    dbias_far = dec_bias(far_hqk)
    dbias_last = dec_bias(_bias_tile(pos_s, past_len - page + jnp.arange(page), rel_bias_table))
    dbias_tail = jnp.stack([dbias_far, dbias_last])
    n_new = -(-ls // BF16_ROWS) * BF16_ROWS
    new_kpos = jnp.where(jnp.arange(n_new) < ls, past_len + jnp.arange(n_new), past_len + 2 * page)
    dbias_new = dec_bias(_bias_tile(pos_s, new_kpos, rel_bias_table))

    outs = {k: [] for k in ('k_p', 'v_p', 'k_s', 'v_s', 'st_p', 'st_s')}
    for i in range(depth):
        j = i // N_MIXERS
        mods = jnp.split(mod[i], 6, axis=-1)
        mp_ = [m[:bp].reshape(bp, 1, d) for m in mods]
        ms_ = [_rows(m[bp:bp + bs], ls) for m in mods]
        g_mix = norm_mix[i].reshape(1, d)
        g_ffn = norm_ffn[i].reshape(1, d)

        if i % N_MIXERS == 0:
            w_in = hgrn_w_in[j].astype(BF16)
            w_out = hgrn_w_out[j].astype(BF16)
            lb = lower_bounds[j].reshape(1, -1)
            onorm = hgrn_out_norm[j].astype(F32).reshape(1, -1)
            dk = onorm.shape[1]
            qfig_p = norm_mod_matmul(xp, g_mix, mp_[0], mp_[1], w_in, lp, ROW_TILE)
            qfig_s = norm_mod_matmul(xs, g_mix, ms_[0], ms_[1], w_in, ls, ms)
            s0_p = jnp.zeros((bp, HG_HEADS, dk, dk), F32)
            op, sp = hgrn_recurrence(qfig_p, lb, onorm, s0_p, bp, lp, BF16)
            os_, ss = hgrn_recurrence(qfig_s, lb, onorm, state_hgrn[j].astype(F32), bs, ls, F32)
            os_ = os_.astype(BF16)
            outs['st_p'].append(sp.astype(state_hgrn.dtype))
            outs['st_s'].append(ss.astype(state_hgrn.dtype))
        else:
            lam_init = 0.8 - 0.6 * math.exp(-0.3 * i)
            lpf = diff_lambda[j].astype(F32)
            lam = (jnp.exp(jnp.sum(lpf[0] * lpf[1])) - jnp.exp(jnp.sum(lpf[2] * lpf[3])) + lam_init).reshape(1)
            w_in = diff_w_in[j].astype(BF16)
            w_out = diff_w_out[j].astype(BF16)
            dh = diff_qk_norm.shape[-1]
            qn = jnp.tile(diff_qk_norm[j, 0], 2).reshape(1, 2 * dh) * (dh ** -0.5)
            kn = jnp.tile(diff_qk_norm[j, 1], 2).reshape(1, 2 * dh)
            subln = diff_subln[j].reshape(1, -1)
            q_p, k_p, v_p = norm_mod_qkv(xp, g_mix, mp_[0], mp_[1], w_in, qn, kn, lp, ROW_TILE)
            q_s, k_s, v_s = norm_mod_qkv(xs, g_mix, ms_[0], ms_[1], w_in, qn, kn, ls, ms)
            op = attn_prompt(q_p, k_p, v_p, lam, bias_far, bias_near, subln, bp, lp, 1.0 - lam_init)
            q4 = q_s.reshape(bs, ls, DF_HEADS, 2 * dh).transpose(0, 2, 1, 3)
            lo = jnp.arange(2 * dh) < dh
            qz = jnp.stack([jnp.where(lo, q4, 0), jnp.where(lo, 0, q4)], axis=2).reshape(bs, DF_HEADS * 2 * ls, 2 * dh)
            k_s5 = k_s.reshape(bs, ls, DF_HEADS, 2 * dh)
            v_s5 = v_s.reshape(bs, ls, DF_HEADS, 2 * dh)
            padw = ((0, 0), (0, n_new - ls), (0, 0), (0, 0))
            o_s = attn_decode(qz, cache_k, cache_v, j, jnp.pad(k_s5, padw), jnp.pad(v_s5, padw), page_table, lam,
                              dbias_far, dbias_tail, dbias_new, subln, 1.0 - lam_init)
            os_ = o_s.transpose(0, 2, 1, 3).reshape(ms, DF_HEADS * 2 * dh).astype(BF16)
            outs['k_p'].append(to_heads(k_p, DF_HEADS, COMBINE_TILE).reshape(bp, lp, DF_HEADS, 2 * dh))
            outs['v_p'].append(to_heads(v_p, DF_HEADS, COMBINE_TILE).reshape(bp, lp, DF_HEADS, 2 * dh))
            outs['k_s'].append(k_s5)
            outs['v_s'].append(v_s5)

        xp = matmul_residual(op, w_out, xp, mp_[2], lp, ROW_TILE)
        xs = matmul_residual(os_, w_out, xs, ms_[2], ls, ms)

        nr = LANES
        wr = jnp.concatenate([router_group_w[i], router_expert_w[i],
                              jnp.zeros((d, nr - N_GROUPS * (1 + EXPERTS_PER_GROUP)), F32)], axis=1)
        wr_hi = wr.astype(BF16)
        wr_lo = (wr - wr_hi.astype(F32)).astype(BF16)
        h_p, lg_p = norm_mod_router(xp, g_ffn, mp_[3], mp_[4], wr_hi, wr_lo, lp, ROW_TILE, mp + ms)
        h_s, lg_s = norm_mod_router(xs, g_ffn, ms_[3], ms_[4], wr_hi, wr_lo, ls, ms, ms)
        h_all = lax.dynamic_update_slice(h_p, h_s, (mp, 0))
        lg_all = lax.dynamic_update_slice(lg_p, lg_s, (mp, 0))
        expert, gate = route(lg_all, router_group_b[i], router_expert_b[i])
        n_hid_tiles = expert_w1.shape[-1] // MOE_HID_TILE
        dest, slot_tok, table = moe_plan(expert, n_experts, n_hid_tiles)
        yb = moe_experts(h_all[slot_tok], table, expert_w1, expert_w3, expert_w2, i)
        dest2 = dest.reshape(mp + ms, TOP_K_IN_GROUP)
        y0, y1 = yb[dest2[:, 0]], yb[dest2[:, 1]]
        g0, g1 = gate[:, 0:1], gate[:, 1:2]
        xp = moe_combine(xp, mp_[5], y0, y1, g0, g1, 0, lp, COMBINE_TILE)
        xs = moe_combine(xs, ms_[5], y0, y1, g0, g1, mp, ls, ms)

    return (xp.reshape(bp, lp, d), xs.reshape(bs, ls, d),
            jnp.stack(outs['k_p']), jnp.stack(outs['v_p']), jnp.stack(outs['k_s']), jnp.stack(outs['v_s']),
            jnp.stack(outs['st_p']), jnp.stack(outs['st_s']))
```
